```python
import jax, jax.numpy as jnp
from jax import lax
import numpy as np

D_MODEL = 2048
BATCH = 4
SEQ = 4096
DEPTH = 2

CHUNK = 64
N_HEADS_A = 8
HEAD_DIM_A = 128
D_ATTN = N_HEADS_A * HEAD_DIM_A
LEFT_CHUNKS = 8
BAND = (LEFT_CHUNKS + 1) * CHUNK
MAX_REL = 4 * CHUNK
N_REL = MAX_REL + CHUNK
N_HEADS_M = 4
HEAD_DIM_M = 256
D_MLSTM = N_HEADS_M * HEAD_DIM_M
CONV_W = 4
D_FF = ((8 * D_MODEL // 3 + 127) // 128) * 128
N_BRANCH = 2
D_IN = 3 * D_ATTN + 4 * D_MLSTM + 2 * N_HEADS_M + N_BRANCH * D_MODEL
EPS = 1e-6

kernel_name = 'hybrid_chunk_attn_mlstm_macaron_adaln'


def rmsnorm(x, g):
    xf = x.astype(jnp.float32)
    y = xf * lax.rsqrt(jnp.mean(xf * xf, axis=-1, keepdims=True) + EPS)
    return (y * g.astype(jnp.float32)).astype(x.dtype)


def modulate(x, g, shift, scale):
    return rmsnorm(x, g) * (1 + scale[:, None, :]) + shift[:, None, :]


def swiglu(h, w1, w3, w2):
    return (jax.nn.silu(h @ w1) * (h @ w3)) @ w2


def causal_dwconv(u, w, b):
    out = lax.conv_general_dilated(
        u, w[:, None, :], window_strides=(1,), padding=((CONV_W - 1, 0),),
        dimension_numbers=('NWC', 'WIO', 'NWC'), feature_group_count=u.shape[-1])
    return out + b


def rel_bias_matrix(table):
    qi = np.arange(CHUNK)[:, None]
    kj = np.arange(BAND)[None, :]
    rel = LEFT_CHUNKS * CHUNK + qi - kj
    idx = np.clip(rel, -(CHUNK - 1), MAX_REL) + (CHUNK - 1)
    return table[:, idx]


def chunk_attention(q, k, v, q_g, k_g, table):
    B, S, _ = q.shape
    NC = S // CHUNK
    shp = (B, NC, CHUNK, N_HEADS_A, HEAD_DIM_A)
    q = rmsnorm(q.reshape(shp), q_g)
    k = rmsnorm(k.reshape(shp), k_g)
    v = v.reshape(shp)
    pad = ((0, 0), (LEFT_CHUNKS, 0), (0, 0), (0, 0), (0, 0))
    kp = jnp.pad(k, pad)
    vp = jnp.pad(v, pad)
    k_band = jnp.concatenate([kp[:, j:j + NC] for j in range(LEFT_CHUNKS + 1)], axis=2)
    v_band = jnp.concatenate([vp[:, j:j + NC] for j in range(LEFT_CHUNKS + 1)], axis=2)
    s = jnp.einsum('bcqhd,bckhd->bhcqk', q, k_band).astype(jnp.float32) * (HEAD_DIM_A ** -0.5)
    s = s + rel_bias_matrix(table).astype(jnp.float32)[None, :, None]
    valid = (np.arange(NC)[:, None] - LEFT_CHUNKS + np.arange(BAND)[None, :] // CHUNK) >= 0
    s = jnp.where(valid[None, None, :, None, :], s, -jnp.inf)
    p = jax.nn.softmax(s, axis=-1).astype(v.dtype)
    o = jnp.einsum('bhcqk,bckhd->bcqhd', p, v_band)
    return o.reshape(B, S, D_ATTN)


def mlstm_chunkwise(q, k, v, i_pre, f_pre):
    B, S, _ = q.shape
    NC = S // CHUNK
    f32 = jnp.float32

    def heads(t):
        return t.astype(f32).reshape(B, NC, CHUNK, N_HEADS_M, HEAD_DIM_M).transpose(0, 3, 1, 2, 4)

    def gates(t):
        return t.astype(f32).reshape(B, NC, CHUNK, N_HEADS_M).transpose(0, 3, 1, 2)

    q = heads(q)
    k = heads(k) * (HEAD_DIM_M ** -0.5)
    v = heads(v)
    ig = gates(i_pre)
    b = jnp.cumsum(jax.nn.log_sigmoid(gates(f_pre)), axis=-1)
    b_last = b[..., -1]
    causal = np.tril(np.ones((CHUNK, CHUNK), dtype=bool))
    d = jnp.where(causal, b[..., :, None] - b[..., None, :] + ig[..., None, :], -jnp.inf)

    w_end = b_last[..., None] - b + ig
    g_end = jnp.max(w_end, axis=-1)
    kw = k * jnp.exp(w_end - g_end[..., None])[..., None]
    a_c = jnp.einsum('bhcsk,bhcsv->bhckv', kw, v)
    a_n = jnp.sum(kw, axis=-2)

    def step(carry, inp):
        C, n, m = carry
        bl, gc, ac, an = inp
        m_new = jnp.maximum(bl + m, gc)
        decay = jnp.exp(bl + m - m_new)
        inj = jnp.exp(gc - m_new)
        C_new = decay[..., None, None] * C + inj[..., None, None] * ac
        n_new = decay[..., None] * n + inj[..., None] * an
        return (C_new, n_new, m_new), (C, n, m)

    init = (jnp.zeros((B, N_HEADS_M, HEAD_DIM_M, HEAD_DIM_M), f32),
            jnp.zeros((B, N_HEADS_M, HEAD_DIM_M), f32),
            jnp.zeros((B, N_HEADS_M), f32))
    xs = (jnp.moveaxis(b_last, 2, 0), jnp.moveaxis(g_end, 2, 0),
          jnp.moveaxis(a_c, 2, 0), jnp.moveaxis(a_n, 2, 0))
    _, (c_prev, n_prev, m_prev) = lax.scan(step, init, xs)
    c_prev = jnp.moveaxis(c_prev, 0, 2)
    n_prev = jnp.moveaxis(n_prev, 0, 2)
    m_prev = jnp.moveaxis(m_prev, 0, 2)

    m_inter = b + m_prev[..., None]
    m_t = jnp.maximum(m_inter, jnp.max(d, axis=-1))
    s = jnp.exp(d - m_t[..., None]) * jnp.einsum('bhctk,bhcsk->bhcts', q, k)
    w_inter = jnp.exp(m_inter - m_t)
    num = (jnp.einsum('bhcts,bhcsv->bhctv', s, v)
           + w_inter[..., None] * jnp.einsum('bhctk,bhckv->bhctv', q, c_prev))
    den = jnp.sum(s, axis=-1) + w_inter * jnp.einsum('bhctk,bhck->bhct', q, n_prev)
    h = num / jnp.maximum(jnp.abs(den), jnp.exp(-m_t))[..., None]
    return h.transpose(0, 2, 3, 1, 4).reshape(B, S, N_HEADS_M, HEAD_DIM_M)


def token_mixer(h, w_in, b_if, conv_w, conv_b, q_norm_g, k_norm_g, rel_table, m_norm_g,
                w_up_a, w_up_m, w_out):
    B, S, _ = h.shape
    z = h @ w_in
    offs = np.cumsum([D_ATTN, D_ATTN, D_ATTN, 2 * D_MLSTM, D_MLSTM, D_MLSTM,
                      2 * N_HEADS_M, D_MODEL]).tolist()
    q_a, k_a, v_a, qk_m, v_m, o_m, if_m, gate_a, gate_m = jnp.split(z, offs, axis=-1)
    attn = chunk_attention(q_a, k_a, v_a, q_norm_g, k_norm_g, rel_table)
    qk_m = jax.nn.silu(causal_dwconv(qk_m, conv_w, conv_b))
    q_m, k_m = jnp.split(qk_m, 2, axis=-1)
    if_m = if_m + b_if
    hm = mlstm_chunkwise(q_m, k_m, v_m, if_m[..., :N_HEADS_M], if_m[..., N_HEADS_M:])
    hm = rmsnorm(hm, m_norm_g.reshape(N_HEADS_M, HEAD_DIM_M)).reshape(B, S, D_MLSTM).astype(h.dtype)
    hm = jax.nn.sigmoid(o_m) * hm
    merged = jax.nn.sigmoid(gate_a) * (attn @ w_up_a) + jax.nn.sigmoid(gate_m) * (hm @ w_up_m)
    return merged @ w_out


def setup_inputs(seed: int = 0) -> dict:
    key = jax.random.key(seed)
    ks = jax.random.split(key, 23)
    f32 = jnp.float32

    def nrm(k, shape, fan_in, scale=1.0):
        return jax.random.normal(k, shape, f32) * (scale * fan_in ** -0.5)

    def gain(k, shape):
        return 1.0 + 0.05 * jax.random.normal(k, shape, f32)

    x = jax.random.normal(ks[0], (BATCH, SEQ, D_MODEL), f32)
    c = jax.random.normal(ks[1], (BATCH, D_MODEL), f32)
    norm_g = gain(ks[2], (DEPTH, 3, D_MODEL))
    w_ada = nrm(ks[3], (DEPTH, D_MODEL, 9 * D_MODEL), D_MODEL, 0.5)
    b_ada = 0.02 * jax.random.normal(ks[4], (DEPTH, 9 * D_MODEL), f32)
    ffn1_w1 = nrm(ks[5], (DEPTH, D_MODEL, D_FF), D_MODEL)
    ffn1_w3 = nrm(ks[6], (DEPTH, D_MODEL, D_FF), D_MODEL)
    ffn1_w2 = nrm(ks[7], (DEPTH, D_FF, D_MODEL), D_FF)
    w_in = nrm(ks[8], (DEPTH, D_MODEL, D_IN), D_MODEL)
    b_i = 0.1 * jax.random.normal(ks[9], (DEPTH, N_HEADS_M), f32)
    b_f = jnp.linspace(3.0, 6.0, N_HEADS_M, dtype=f32)[None, :] + 0.1 * jax.random.normal(ks[10], (DEPTH, N_HEADS_M), f32)
    b_if = jnp.concatenate([b_i, b_f], axis=-1)
    conv_w = nrm(ks[11], (DEPTH, CONV_W, 2 * D_MLSTM), CONV_W)
    conv_b = 0.02 * jax.random.normal(ks[12], (DEPTH, 2 * D_MLSTM), f32)
    q_norm_g = gain(ks[13], (DEPTH, HEAD_DIM_A))
    k_norm_g = gain(ks[14], (DEPTH, HEAD_DIM_A))
    rel_table = 0.2 * jax.random.normal(ks[15], (DEPTH, N_HEADS_A, N_REL), f32)
    m_norm_g = gain(ks[16], (DEPTH, D_MLSTM))
    w_up_a = nrm(ks[17], (DEPTH, D_ATTN, D_MODEL), D_ATTN)
    w_up_m = nrm(ks[18], (DEPTH, D_MLSTM, D_MODEL), D_MLSTM)
    w_out = nrm(ks[19], (DEPTH, D_MODEL, D_MODEL), D_MODEL)
    ffn2_w1 = nrm(ks[20], (DEPTH, D_MODEL, D_FF), D_MODEL)
    ffn2_w3 = nrm(ks[21], (DEPTH, D_MODEL, D_FF), D_MODEL)
    ffn2_w2 = nrm(ks[22], (DEPTH, D_FF, D_MODEL), D_FF)
    return {'x': x, 'c': c, 'norm_g': norm_g, 'w_ada': w_ada, 'b_ada': b_ada,
            'ffn1_w1': ffn1_w1, 'ffn1_w3': ffn1_w3, 'ffn1_w2': ffn1_w2,
            'w_in': w_in, 'b_if': b_if, 'conv_w': conv_w, 'conv_b': conv_b,
            'q_norm_g': q_norm_g, 'k_norm_g': k_norm_g, 'rel_table': rel_table,
            'm_norm_g': m_norm_g, 'w_up_a': w_up_a, 'w_up_m': w_up_m, 'w_out': w_out,
            'ffn2_w1': ffn2_w1, 'ffn2_w3': ffn2_w3, 'ffn2_w2': ffn2_w2}


def reference(x, c, norm_g, w_ada, b_ada, ffn1_w1, ffn1_w3, ffn1_w2, w_in, b_if, conv_w, conv_b,
              q_norm_g, k_norm_g, rel_table, m_norm_g, w_up_a, w_up_m, w_out,
              ffn2_w1, ffn2_w3, ffn2_w2):
    c_act = jax.nn.silu(c)
    for l in range(DEPTH):
        mod = c_act @ w_ada[l] + b_ada[l]
        sh1, sc1, g1, sh2, sc2, g2, sh3, sc3, g3 = jnp.split(mod, 9, axis=-1)
        x = x + 0.5 * g1[:, None, :] * swiglu(modulate(x, norm_g[l, 0], sh1, sc1),
                                              ffn1_w1[l], ffn1_w3[l], ffn1_w2[l])
        x = x + g2[:, None, :] * token_mixer(modulate(x, norm_g[l, 1], sh2, sc2),
                                             w_in[l], b_if[l], conv_w[l], conv_b[l],
                                             q_norm_g[l], k_norm_g[l], rel_table[l], m_norm_g[l],
                                             w_up_a[l], w_up_m[l], w_out[l])
        x = x + 0.5 * g3[:, None, :] * swiglu(modulate(x, norm_g[l, 2], sh3, sc3),
                                              ffn2_w1[l], ffn2_w3[l], ffn2_w2[l])
    return x
```

```python
import functools

import numpy as np
import jax
import jax.numpy as jnp
from jax import lax
from jax.experimental import pallas as pl
from jax.experimental.pallas import tpu as pltpu

F32 = jnp.float32
BF16 = jnp.bfloat16

D_MODEL = 2048
CHUNK = 64
N_HEADS_A = 8
HEAD_DIM_A = 128
D_ATTN = N_HEADS_A * HEAD_DIM_A
LEFT_CHUNKS = 8
LEFT = LEFT_CHUNKS * CHUNK
MAX_REL = 4 * CHUNK
N_HEADS_M = 4
HEAD_DIM_M = 256
D_MLSTM = N_HEADS_M * HEAD_DIM_M
CONV_W = 4
D_FF = ((8 * D_MODEL // 3 + 127) // 128) * 128
EPS = 1e-6
NEG = -1e30

Z_QA, Z_KA, Z_VA = 0, D_ATTN, 2 * D_ATTN
Z_QM = 3 * D_ATTN
Z_KM = Z_QM + D_MLSTM
Z_VM = Z_KM + D_MLSTM
Z_OM = Z_VM + D_MLSTM
Z_GA = Z_OM + D_MLSTM
Z_GM = Z_GA + D_MODEL
Z_COLS = Z_GM + D_MODEL
IF_COLS = 128

FFN_TM, FFN_TF = 512, 512
D_FF_PAD = ((D_FF + FFN_TF - 1) // FFN_TF) * FFN_TF
PROJ_TM, PROJ_TN = 512, 1024
ATT_TQ = 128
ATT_W = LEFT + ATT_TQ
ATT_HG = 4
ML_L = 256
MERGE_TM, MERGE_TN = 512, 512
ADA_TN = 1024
VMEM_LIMIT = 48 * 1024 * 1024


def _params(sem):
    return pltpu.CompilerParams(dimension_semantics=sem, vmem_limit_bytes=VMEM_LIMIT)


def _modulated(x, g, mod_ref, row0):
    y = x * lax.rsqrt(jnp.mean(x * x, axis=-1, keepdims=True) + EPS)
    return (y * g) * (1.0 + mod_ref[row0 + 1:row0 + 2, :]) + mod_ref[row0:row0 + 1, :]


def _ada_kernel(c_ref, w_ref, b_ref, o_ref):
    c = c_ref[...]
    ca = (c * jax.nn.sigmoid(c)).astype(BF16)
    o_ref[...] = jnp.dot(ca, w_ref[...].astype(BF16), preferred_element_type=F32) + b_ref[...]


def _ada(c8, w_ada, b_ada):
    depth, d, n = w_ada.shape
    return pl.pallas_call(
        _ada_kernel,
        out_shape=jax.ShapeDtypeStruct((depth, 8, n), F32),
        grid=(depth, n // ADA_TN),
        in_specs=[pl.BlockSpec((8, d), lambda l, j: (0, 0)),
                  pl.BlockSpec((None, d, ADA_TN), lambda l, j: (l, 0, j)),
                  pl.BlockSpec((None, 1, ADA_TN), lambda l, j: (l, 0, j))],
        out_specs=pl.BlockSpec((None, 8, ADA_TN), lambda l, j: (l, 0, j)),
        compiler_params=_params(("parallel", "parallel")),
        name="adaln",
    )(c8, w_ada, b_ada.reshape(depth, 1, n))


def _ffn_kernel(x_ref, g_ref, mod_ref, w1_ref, w3_ref, w2_ref, o_ref, h_ref, acc_ref, *, row0):
    j = pl.program_id(1)

    @pl.when(j == 0)
    def _():
        h_ref[...] = _modulated(x_ref[...], g_ref[...], mod_ref, row0).astype(BF16)
        acc_ref[...] = jnp.zeros_like(acc_ref)

    h = h_ref[...]
    a = jnp.dot(h, w1_ref[...], preferred_element_type=F32)
    b = jnp.dot(h, w3_ref[...], preferred_element_type=F32)
    u = (a * jax.nn.sigmoid(a)) * b
    acc_ref[...] += jnp.dot(u.astype(BF16), w2_ref[...], preferred_element_type=F32)

    @pl.when(j == pl.num_programs(1) - 1)
    def _():
        o_ref[...] = x_ref[...] + (0.5 * mod_ref[row0 + 2:row0 + 3, :]) * acc_ref[...]


def _ffn(x2, g, mod, w1, w3, w2, row0, seq):
    t, d = x2.shape
    tpb = seq // FFN_TM
    nf = w1.shape[1] // FFN_TF
    return pl.pallas_call(
        functools.partial(_ffn_kernel, row0=row0),
        out_shape=jax.ShapeDtypeStruct((t, d), F32),
        grid=(t // FFN_TM, nf),
        in_specs=[pl.BlockSpec((FFN_TM, d), lambda i, j: (i, 0)),
                  pl.BlockSpec((1, d), lambda i, j: (0, 0)),
                  pl.BlockSpec((None, 9, d), lambda i, j: (i // tpb, 0, 0)),
                  pl.BlockSpec((d, FFN_TF), lambda i, j: (0, j)),
                  pl.BlockSpec((d, FFN_TF), lambda i, j: (0, j)),
                  pl.BlockSpec((FFN_TF, d), lambda i, j: (j, 0))],
        out_specs=pl.BlockSpec((FFN_TM, d), lambda i, j: (i, 0)),
        scratch_shapes=[pltpu.VMEM((FFN_TM, d), BF16), pltpu.VMEM((FFN_TM, d), F32)],
        compiler_params=_params(("parallel", "arbitrary")),
        name="ffn",
    )(x2, g, mod, w1, w3, w2)


def _proj_kernel(x_ref, g_ref, mod_ref, w_ref, wif_ref, bif_ref, qg_ref, kg_ref, z_ref, zif_ref, h_ref, *, row0):
    j = pl.program_id(1)

    @pl.when(j == 0)
    def _():
        hb = _modulated(x_ref[...], g_ref[...], mod_ref, row0).astype(BF16)
        h_ref[...] = hb
        zif_ref[...] = jnp.dot(hb, wif_ref[...], preferred_element_type=F32) + bif_ref[...]

    acc = jnp.dot(h_ref[...], w_ref[...], preferred_element_type=F32)

    def head_norm(gain_ref, scale):
        for hd in range(PROJ_TN // HEAD_DIM_A):
            a = acc[:, hd * HEAD_DIM_A:(hd + 1) * HEAD_DIM_A]
            y = a * lax.rsqrt(jnp.mean(a * a, axis=-1, keepdims=True) + EPS)
            z_ref[:, hd * HEAD_DIM_A:(hd + 1) * HEAD_DIM_A] = ((y * gain_ref[...]) * scale).astype(BF16)

    jq, jk = Z_QA // PROJ_TN, Z_KA // PROJ_TN
    j_sig = Z_OM // PROJ_TN

    @pl.when(j == jq)
    def _():
        head_norm(qg_ref, HEAD_DIM_A ** -0.5)

    @pl.when(j == jk)
    def _():
        head_norm(kg_ref, 1.0)

    @pl.when(jnp.logical_and(j > jk, j < j_sig))
    def _():
        z_ref[...] = acc.astype(BF16)

    @pl.when(j >= j_sig)
    def _():
        z_ref[...] = jax.nn.sigmoid(acc).astype(BF16)


def _proj(x2, g, mod, w_p, w_if, b_if, q_g, k_g, row0, seq):
    t, d = x2.shape
    tpb = seq // PROJ_TM
    return pl.pallas_call(
        functools.partial(_proj_kernel, row0=row0),
        out_shape=(jax.ShapeDtypeStruct((t, Z_COLS), BF16), jax.ShapeDtypeStruct((t, IF_COLS), F32)),
        grid=(t // PROJ_TM, Z_COLS // PROJ_TN),
        in_specs=[pl.BlockSpec((PROJ_TM, d), lambda i, j: (i, 0)),
                  pl.BlockSpec((1, d), lambda i, j: (0, 0)),
                  pl.BlockSpec((None, 9, d), lambda i, j: (i // tpb, 0, 0)),
                  pl.BlockSpec((d, PROJ_TN), lambda i, j: (0, j)),
                  pl.BlockSpec((d, IF_COLS), lambda i, j: (0, 0)),
                  pl.BlockSpec((1, IF_COLS), lambda i, j: (0, 0)),
                  pl.BlockSpec((1, HEAD_DIM_A), lambda i, j: (0, 0)),
                  pl.BlockSpec((1, HEAD_DIM_A), lambda i, j: (0, 0))],
        out_specs=(pl.BlockSpec((PROJ_TM, PROJ_TN), lambda i, j: (i, j)),
                   pl.BlockSpec((PROJ_TM, IF_COLS), lambda i, j: (i, 0))),
        scratch_shapes=[pltpu.VMEM((PROJ_TM, d), BF16)],
        compiler_params=_params(("parallel", "arbitrary")),
        name="proj",
    )(x2, g, mod, w_p, w_if, b_if, q_g, k_g)


def _attn_bias(table):
    r = np.arange(ATT_TQ)[:, None]
    c = np.arange(ATT_W)[None, :]
    out = []
    for v in range(LEFT // ATT_TQ + 1):
        shift = v * ATT_TQ
        idx = np.clip(shift + r - c, -(CHUNK - 1), MAX_REL) + (CHUNK - 1)
        dq = (shift + r) // CHUNK - c // CHUNK
        valid = (dq >= 0) & (dq <= LEFT_CHUNKS)
        out.append(jnp.where(valid[None], table[:, idx], NEG))
    return jnp.stack(out, axis=1)


def _attn_kernel(q_ref, k_ref, v_ref, bias_ref, o_ref):
    qi = pl.program_id(2)
    var = jnp.minimum(qi, LEFT // ATT_TQ)
    ks = pl.multiple_of((qi - var) * ATT_TQ, ATT_TQ)
    for hd in range(ATT_HG):
        cols = slice(hd * HEAD_DIM_A, (hd + 1) * HEAD_DIM_A)
        q = q_ref[:, cols]
        k = k_ref[pl.ds(ks, ATT_W), cols]
        v = v_ref[pl.ds(ks, ATT_W), cols]
        s = lax.dot_general(q, k, (((1,), (1,)), ((), ())), preferred_element_type=F32)
        s = s + bias_ref[hd, var]
        p = jnp.exp(s - jnp.max(s, axis=-1, keepdims=True))
        l = jnp.sum(p, axis=-1, keepdims=True)
        o = jnp.dot(p.astype(BF16), v, preferred_element_type=F32)
        o_ref[:, cols] = (o / l).astype(BF16)


def _attention(z3, bias):
    b, s, _ = z3.shape
    gw = ATT_HG * HEAD_DIM_A
    nvar = bias.shape[1]
    return pl.pallas_call(
        _attn_kernel,
        out_shape=jax.ShapeDtypeStruct((b, s, D_ATTN), BF16),
        grid=(b, N_HEADS_A // ATT_HG, s // ATT_TQ),
        in_specs=[pl.BlockSpec((None, ATT_TQ, gw), lambda bi, g, qi: (bi, qi, Z_QA // gw + g)),
                  pl.BlockSpec((None, s, gw), lambda bi, g, qi: (bi, 0, Z_KA // gw + g)),
                  pl.BlockSpec((None, s, gw), lambda bi, g, qi: (bi, 0, Z_VA // gw + g)),
                  pl.BlockSpec((ATT_HG, nvar, ATT_TQ, ATT_W), lambda bi, g, qi: (g, 0, 0, 0))],
        out_specs=pl.BlockSpec((None, ATT_TQ, gw), lambda bi, g, qi: (bi, qi, g)),
        compiler_params=_params(("parallel", "parallel", "arbitrary")),
        name="attn",
    )(z3, z3, z3, bias)


def _mlstm_kernel(uq_ref, uk_ref, v_ref, o_ref, zif_ref, cwq_ref, cwk_ref, cbq_ref, cbk_ref, g_ref, out_ref,
                  c_ref, n_ref, m_ref, eq_ref, ek_ref):
    L = ML_L
    ci = pl.program_id(1)

    @pl.when(ci == 0)
    def _():
        c_ref[...] = jnp.zeros_like(c_ref)
        n_ref[...] = jnp.zeros_like(n_ref)
        m_ref[...] = jnp.zeros_like(m_ref)
        eq_ref[0:8, :] = jnp.zeros((8, D_MLSTM), F32)
        ek_ref[0:8, :] = jnp.zeros((8, D_MLSTM), F32)

    def conv_silu(u_ref, e_ref, cw_ref, cb_ref):
        e_ref[8:8 + L, :] = u_ref[...].astype(F32)
        acc = cb_ref[...] + e_ref[8:8 + L, :] * cw_ref[CONV_W - 1:CONV_W, :]
        for dlt in range(1, CONV_W):
            acc = acc + e_ref[8 - dlt:8 - dlt + L, :] * cw_ref[CONV_W - 1 - dlt:CONV_W - dlt, :]
        e_ref[0:8, :] = e_ref[L:L + 8, :]
        return acc * jax.nn.sigmoid(acc)

    q_all = conv_silu(uq_ref, eq_ref, cwq_ref, cbq_ref)
    k_all = conv_silu(uk_ref, ek_ref, cwk_ref, cbk_ref) * (HEAD_DIM_M ** -0.5)

    zi = zif_ref[...]
    logf = jnp.minimum(zi, 0.0) - jnp.log1p(jnp.exp(-jnp.abs(zi)))
    row = lax.broadcasted_iota(jnp.int32, (L, L), 0)
    col = lax.broadcasted_iota(jnp.int32, (L, L), 1)
    causal = row >= col
    bcs = jnp.dot(causal.astype(F32), logf, preferred_element_type=F32,
                  precision=lax.Precision.HIGHEST)
    zi_t = zi.T
    bcs_t = bcs.T

    for hd in range(N_HEADS_M):
        cols = slice(hd * HEAD_DIM_M, (hd + 1) * HEAD_DIM_M)
        q = q_all[:, cols]
        k = k_all[:, cols]
        qb = q.astype(BF16)
        kb = k.astype(BF16)
        v = v_ref[:, cols]
        i_col = zi[:, hd:hd + 1]
        b_col = bcs[:, N_HEADS_M + hd:N_HEADS_M + hd + 1]
        i_row = zi_t[hd:hd + 1, :]
        b_row = bcs_t[N_HEADS_M + hd:N_HEADS_M + hd + 1, :]
        m_prev = m_ref[hd, 0:1, 0:1]
        c_prev = c_ref[hd]
        n_prev = n_ref[hd]

        d = jnp.where(causal, b_col + (i_row - b_row), NEG)
        m_inter = b_col + m_prev
        m_t = jnp.maximum(m_inter, jnp.max(d, axis=-1, keepdims=True))
        qk = lax.dot_general(qb, kb, (((1,), (1,)), ((), ())), preferred_element_type=F32)
        sm = jnp.exp(d - m_t) * qk
        w_inter = jnp.exp(m_inter - m_t)
        num = (jnp.dot(sm.astype(BF16), v, preferred_element_type=F32)
               + w_inter * jnp.dot(qb, c_prev.astype(BF16), preferred_element_type=F32))
        den = (jnp.sum(sm, axis=-1, keepdims=True)
               + w_inter * jnp.sum(q * n_prev, axis=-1, keepdims=True))
        hh = num / jnp.maximum(jnp.abs(den), jnp.exp(-m_t))

        b_last = b_col[L - 1:L, :]
        w_end = b_last - b_col + i_col
        g_end = jnp.max(w_end, axis=0, keepdims=True)
        m_new = jnp.maximum(b_last + m_prev, g_end)
        decay = jnp.exp(b_last + m_prev - m_new)
        inj = jnp.exp(g_end - m_new)
        kw = k * jnp.exp(w_end - g_end)
        a_c = lax.dot_general(kw.astype(BF16), v, (((0,), (0,)), ((), ())), preferred_element_type=F32)
        c_ref[hd] = decay * c_prev + inj * a_c
        n_ref[hd] = decay * n_prev + inj * jnp.sum(kw, axis=0, keepdims=True)
        m_ref[hd] = jnp.broadcast_to(m_new, m_ref.shape[1:])

        hn = hh * lax.rsqrt(jnp.mean(hh * hh, axis=-1, keepdims=True) + EPS) * g_ref[:, cols]
        out_ref[:, cols] = (o_ref[:, cols].astype(F32) * hn).astype(BF16)


def _mlstm(z3, zif3, conv_w, conv_b, m_g):
    b, s, _ = z3.shape
    L = ML_L
    dm = D_MLSTM
    return pl.pallas_call(
        _mlstm_kernel,
        out_shape=jax.ShapeDtypeStruct((b, s, dm), BF16),
        grid=(b, s // L),
        in_specs=[pl.BlockSpec((None, L, dm), lambda bi, ci: (bi, ci, Z_QM // dm)),
                  pl.BlockSpec((None, L, dm), lambda bi, ci: (bi, ci, Z_KM // dm)),
                  pl.BlockSpec((None, L, dm), lambda bi, ci: (bi, ci, Z_VM // dm)),
                  pl.BlockSpec((None, L, dm), lambda bi, ci: (bi, ci, Z_OM // dm)),
                  pl.BlockSpec((None, L, IF_COLS), lambda bi, ci: (bi, ci, 0)),
                  pl.BlockSpec((CONV_W, dm), lambda bi, ci: (0, 0)),
                  pl.BlockSpec((CONV_W, dm), lambda bi, ci: (0, 1)),
                  pl.BlockSpec((1, dm), lambda bi, ci: (0, 0)),
                  pl.BlockSpec((1, dm), lambda bi, ci: (0, 1)),
                  pl.BlockSpec((1, dm), lambda bi, ci: (0, 0))],
        out_specs=pl.BlockSpec((None, L, dm), lambda bi, ci: (bi, ci, 0)),
        scratch_shapes=[pltpu.VMEM((N_HEADS_M, HEAD_DIM_M, HEAD_DIM_M), F32),
                        pltpu.VMEM((N_HEADS_M, 1, HEAD_DIM_M), F32),
                        pltpu.VMEM((N_HEADS_M, 8, 128), F32),
                        pltpu.VMEM((L + 8, dm), F32),
                        pltpu.VMEM((L + 8, dm), F32)],
        compiler_params=_params(("parallel", "arbitrary")),
        name="mlstm",
    )(z3, z3, z3, z3, zif3, conv_w, conv_w, conv_b, conv_b, m_g)


def _merge_kernel(x_ref, mod_ref, a_ref, hm_ref, ga_ref, gm_ref, wa_ref, wm_ref, wo_ref, o_ref, acc_ref, *, row0):
    j = pl.program_id(1)

    @pl.when(j == 0)
    def _():
        acc_ref[...] = jnp.zeros_like(acc_ref)

    ua = jnp.dot(a_ref[...], wa_ref[...], preferred_element_type=F32)
    um = jnp.dot(hm_ref[...], wm_ref[...], preferred_element_type=F32)
    merged = ga_ref[...].astype(F32) * ua + gm_ref[...].astype(F32) * um
    acc_ref[...] += jnp.dot(merged.astype(BF16), wo_ref[...], preferred_element_type=F32)

    @pl.when(j == pl.num_programs(1) - 1)
    def _():
        o_ref[...] = x_ref[...] + mod_ref[row0 + 2:row0 + 3, :] * acc_ref[...]


def _merge(x2, mod, attn2, hm2, z2, w_up_a, w_up_m, w_out, row0, seq):
    t, d = x2.shape
    tpb = seq // MERGE_TM
    tn = MERGE_TN
    return pl.pallas_call(
        functools.partial(_merge_kernel, row0=row0),
        out_shape=jax.ShapeDtypeStruct((t, d), F32),
        grid=(t // MERGE_TM, d // tn),
        in_specs=[pl.BlockSpec((MERGE_TM, d), lambda i, j: (i, 0)),
                  pl.BlockSpec((None, 9, d), lambda i, j: (i // tpb, 0, 0)),
                  pl.BlockSpec((MERGE_TM, D_ATTN), lambda i, j: (i, 0)),
                  pl.BlockSpec((MERGE_TM, D_MLSTM), lambda i, j: (i, 0)),
                  pl.BlockSpec((MERGE_TM, tn), lambda i, j: (i, Z_GA // tn + j)),
                  pl.BlockSpec((MERGE_TM, tn), lambda i, j: (i, Z_GM // tn + j)),
                  pl.BlockSpec((D_ATTN, tn), lambda i, j: (0, j)),
                  pl.BlockSpec((D_MLSTM, tn), lambda i, j: (0, j)),
                  pl.BlockSpec((tn, d), lambda i, j: (j, 0))],
        out_specs=pl.BlockSpec((MERGE_TM, d), lambda i, j: (i, 0)),
        scratch_shapes=[pltpu.VMEM((MERGE_TM, d), F32)],
        compiler_params=_params(("parallel", "arbitrary")),
        name="merge",
    )(x2, mod, attn2, hm2, z2, z2, w_up_a, w_up_m, w_out)


def kernel(x, c, norm_g, w_ada, b_ada, ffn1_w1, ffn1_w3, ffn1_w2, w_in, b_if, conv_w, conv_b, q_norm_g, k_norm_g,
           rel_table, m_norm_g, w_up_a, w_up_m, w_out, ffn2_w1, ffn2_w3, ffn2_w2):
    bsz, seq, d = x.shape
    depth = w_ada.shape[0]
    t = bsz * seq
    assert d == D_MODEL and seq % FFN_TM == 0 and seq % ML_L == 0 and seq >= ATT_W and bsz <= 8

    c8 = jnp.zeros((8, d), F32).at[:bsz].set(c)
    mod_all = _ada(c8, w_ada, b_ada)[:, :bsz].reshape(depth, bsz, 9, d)

    def ffn_weights(w1, w3, w2):
        pad = D_FF_PAD - D_FF
        return (jnp.pad(w1.astype(BF16), ((0, 0), (0, pad))), jnp.pad(w3.astype(BF16), ((0, 0), (0, pad))),
                jnp.pad(w2.astype(BF16), ((0, pad), (0, 0))))

    n_if = 2 * N_HEADS_M
    if0 = Z_OM + D_MLSTM
    x2 = x.reshape(t, d)
    for l in range(depth):
        mod = mod_all[l]
        x2 = _ffn(x2, norm_g[l, 0:1], mod, *ffn_weights(ffn1_w1[l], ffn1_w3[l], ffn1_w2[l]), 0, seq)

        w_p = jnp.concatenate([w_in[l, :, :if0], w_in[l, :, if0 + n_if:]], axis=1).astype(BF16)
        w_if = jnp.pad(w_in[l, :, if0:if0 + n_if], ((0, 0), (0, IF_COLS - n_if))).astype(BF16)
        bif = jnp.pad(b_if[l], (0, IF_COLS - n_if)).reshape(1, IF_COLS)
        z2, zif2 = _proj(x2, norm_g[l, 1:2], mod, w_p, w_if, bif, q_norm_g[l].reshape(1, -1),
                         k_norm_g[l].reshape(1, -1), 3, seq)
        z3 = z2.reshape(bsz, seq, Z_COLS)
        attn = _attention(z3, _attn_bias(rel_table[l]))
        hm = _mlstm(z3, zif2.reshape(bsz, seq, IF_COLS), conv_w[l], conv_b[l].reshape(1, -1),
                    m_norm_g[l].reshape(1, -1))
        x2 = _merge(x2, mod, attn.reshape(t, D_ATTN), hm.reshape(t, D_MLSTM), z2, w_up_a[l].astype(BF16),
                    w_up_m[l].astype(BF16), w_out[l].astype(BF16), 3, seq)

        x2 = _ffn(x2, norm_g[l, 2:3], mod, *ffn_weights(ffn2_w1[l], ffn2_w3[l], ffn2_w2[l]), 6, seq)
    return x2.reshape(bsz, seq, d)
```

```python
import functools

import numpy as np
import jax
import jax.numpy as jnp
from jax import lax
from jax.experimental import pallas as pl
from jax.experimental.pallas import tpu as pltpu

F32 = jnp.float32
BF16 = jnp.bfloat16

D_MODEL = 2048
CHUNK = 64
N_HEADS_A = 8
HEAD_DIM_A = 128
D_ATTN = N_HEADS_A * HEAD_DIM_A
LEFT_CHUNKS = 8
LEFT = LEFT_CHUNKS * CHUNK
MAX_REL = 4 * CHUNK
N_HEADS_M = 4
HEAD_DIM_M = 256
D_MLSTM = N_HEADS_M * HEAD_DIM_M
CONV_W = 4
D_FF = ((8 * D_MODEL // 3 + 127) // 128) * 128
EPS = 1e-6
NEG = -1e30

Z_QA, Z_KA, Z_VA = 0, D_ATTN, 2 * D_ATTN
Z_QM = 3 * D_ATTN
Z_KM = Z_QM + D_MLSTM
Z_VM = Z_KM + D_MLSTM
Z_OM = Z_VM + D_MLSTM
Z_GA = Z_OM + D_MLSTM
Z_GM = Z_GA + D_MODEL
Z_COLS = Z_GM + D_MODEL
IF_COLS = 128
N_IF = 2 * N_HEADS_M
IF0 = Z_GA

FFN_TM, FFN_TF = 512, 512
D_FF_PAD = ((D_FF + FFN_TF - 1) // FFN_TF) * FFN_TF
PROJ_TM, PROJ_TN = 512, 1024
ATT_TQ = 128
ATT_W = LEFT + ATT_TQ
ATT_HG = 4
ML_L = 256
MERGE_TM, MERGE_TN = 512, 512
ADA_TN = 1024
WIN_TN = 1024
CAST_B = 512
VMEM_LIMIT = 48 * 1024 * 1024


def _params(sem):
    return pltpu.CompilerParams(dimension_semantics=sem, vmem_limit_bytes=VMEM_LIMIT)


def _modulated(x, g, mod_ref, row0):
    y = x * lax.rsqrt(jnp.mean(x * x, axis=-1, keepdims=True) + EPS)
    return (y * g) * (1.0 + mod_ref[row0 + 1:row0 + 2, :]) + mod_ref[row0:row0 + 1, :]


def _ada_kernel(c_ref, w_ref, b_ref, o_ref):
    c = c_ref[...]
    ca = (c * jax.nn.sigmoid(c)).astype(BF16)
    o_ref[...] = jnp.dot(ca, w_ref[...].astype(BF16), preferred_element_type=F32) + b_ref[...]


def _ada(c8, w_ada, b_ada):
    depth, d, n = w_ada.shape
    return pl.pallas_call(
        _ada_kernel,
        out_shape=jax.ShapeDtypeStruct((depth, 8, n), F32),
        grid=(depth, n // ADA_TN),
        in_specs=[pl.BlockSpec((8, d), lambda l, j: (0, 0)),
                  pl.BlockSpec((None, d, ADA_TN), lambda l, j: (l, 0, j)),
                  pl.BlockSpec((None, 1, ADA_TN), lambda l, j: (l, 0, j))],
        out_specs=pl.BlockSpec((None, 8, ADA_TN), lambda l, j: (l, 0, j)),
        compiler_params=_params(("parallel", "parallel")),
        name="adaln",
    )(c8, w_ada, b_ada.reshape(depth, 1, n))


def _cast_kernel(w_ref, o_ref, *, rows, cols):
    br, bc = o_ref.shape
    x = w_ref[...]
    if rows % br or cols % bc:
        r = lax.broadcasted_iota(jnp.int32, (br, bc), 0) + pl.program_id(1) * br
        c = lax.broadcasted_iota(jnp.int32, (br, bc), 1) + pl.program_id(2) * bc
        x = jnp.where(jnp.logical_and(r < rows, c < cols), x, 0.0)
    o_ref[...] = x.astype(BF16)


def _cast_pad(w, rows_out, cols_out, br, bc):
    depth, rows, cols = w.shape
    return pl.pallas_call(
        functools.partial(_cast_kernel, rows=rows, cols=cols),
        out_shape=jax.ShapeDtypeStruct((depth, rows_out, cols_out), BF16),
        grid=(depth, rows_out // br, cols_out // bc),
        in_specs=[pl.BlockSpec((None, br, bc), lambda l, i, j: (l, i, j))],
        out_specs=pl.BlockSpec((None, br, bc), lambda l, i, j: (l, i, j)),
        compiler_params=_params(("parallel", "parallel", "parallel")),
        name="cast",
    )(w)


def _cast_win_kernel(a_ref, b_ref, wp_ref, wif_ref):
    j = pl.program_id(1)
    j_if = IF0 // WIN_TN

    @pl.when(j < j_if)
    def _():
        wp_ref[...] = a_ref[...].astype(BF16)

    @pl.when(j >= j_if)
    def _():
        full = jnp.concatenate([a_ref[...], b_ref[...]], axis=1)
        wp_ref[...] = pltpu.roll(full, full.shape[1] - N_IF, axis=1)[:, :WIN_TN].astype(BF16)

    @pl.when(j == j_if)
    def _():
        lane = lax.broadcasted_iota(jnp.int32, wif_ref.shape, 1)
        wif_ref[...] = jnp.where(lane < N_IF, a_ref[:, :IF_COLS], 0.0).astype(BF16)


def _cast_win(w_in):
    depth, d, _ = w_in.shape
    per = WIN_TN // IF_COLS
    j_if = IF0 // WIN_TN
    return pl.pallas_call(
        _cast_win_kernel,
        out_shape=(jax.ShapeDtypeStruct((depth, d, Z_COLS), BF16), jax.ShapeDtypeStruct((depth, d, IF_COLS), BF16)),
        grid=(depth, Z_COLS // WIN_TN),
        in_specs=[pl.BlockSpec((None, d, WIN_TN), lambda l, j: (l, 0, j)),
                  pl.BlockSpec((None, d, IF_COLS), lambda l, j: (l, 0, (jnp.maximum(j, j_if) + 1) * per))],
        out_specs=(pl.BlockSpec((None, d, WIN_TN), lambda l, j: (l, 0, j)),
                   pl.BlockSpec((None, d, IF_COLS), lambda l, j: (l, 0, 0))),
        compiler_params=_params(("parallel", "arbitrary")),
        name="cast_win",
    )(w_in, w_in)


def _ffn_kernel(x_ref, g_ref, mod_ref, w1_ref, w3_ref, w2_ref, o_ref, h_ref, acc_ref, *, row0):
    j = pl.program_id(1)

    @pl.when(j == 0)
    def _():
        h_ref[...] = _modulated(x_ref[...], g_ref[...], mod_ref, row0).astype(BF16)
        acc_ref[...] = jnp.zeros_like(acc_ref)

    h = h_ref[...]
    a = jnp.dot(h, w1_ref[...], preferred_element_type=F32)
    b = jnp.dot(h, w3_ref[...], preferred_element_type=F32)
    u = (a * jax.nn.sigmoid(a)) * b
    acc_ref[...] += jnp.dot(u.astype(BF16), w2_ref[...], preferred_element_type=F32)

    @pl.when(j == pl.num_programs(1) - 1)
    def _():
        o_ref[...] = x_ref[...] + (0.5 * mod_ref[row0 + 2:row0 + 3, :]) * acc_ref[...]


def _ffn(x2, g, mod, w1, w3, w2, layer, row0, seq):
    t, d = x2.shape
    tpb = seq // FFN_TM
    nf = w1.shape[2] // FFN_TF
    return pl.pallas_call(
        functools.partial(_ffn_kernel, row0=row0),
        out_shape=jax.ShapeDtypeStruct((t, d), F32),
        grid=(t // FFN_TM, nf),
        in_specs=[pl.BlockSpec((FFN_TM, d), lambda i, j: (i, 0)),
                  pl.BlockSpec((1, d), lambda i, j: (0, 0)),
                  pl.BlockSpec((None, 9, d), lambda i, j: (i // tpb, 0, 0)),
                  pl.BlockSpec((None, d, FFN_TF), lambda i, j: (layer, 0, j)),
                  pl.BlockSpec((None, d, FFN_TF), lambda i, j: (layer, 0, j)),
                  pl.BlockSpec((None, FFN_TF, d), lambda i, j: (layer, j, 0))],
        out_specs=pl.BlockSpec((FFN_TM, d), lambda i, j: (i, 0)),
        scratch_shapes=[pltpu.VMEM((FFN_TM, d), BF16), pltpu.VMEM((FFN_TM, d), F32)],
        compiler_params=_params(("parallel", "arbitrary")),
        name="ffn",
    )(x2, g, mod, w1, w3, w2)


def _proj_kernel(x_ref, g_ref, mod_ref, w_ref, wif_ref, bif_ref, qg_ref, kg_ref, z_ref, zif_ref, h_ref, *, row0):
    j = pl.program_id(1)

    @pl.when(j == 0)
    def _():
        hb = _modulated(x_ref[...], g_ref[...], mod_ref, row0).astype(BF16)
        h_ref[...] = hb
        zif_ref[...] = jnp.dot(hb, wif_ref[...], preferred_element_type=F32) + bif_ref[...]

    acc = jnp.dot(h_ref[...], w_ref[...], preferred_element_type=F32)

    def head_norm(gain_ref, scale):
        for hd in range(PROJ_TN // HEAD_DIM_A):
            a = acc[:, hd * HEAD_DIM_A:(hd + 1) * HEAD_DIM_A]
            y = a * lax.rsqrt(jnp.mean(a * a, axis=-1, keepdims=True) + EPS)
            z_ref[:, hd * HEAD_DIM_A:(hd + 1) * HEAD_DIM_A] = ((y * gain_ref[...]) * scale).astype(BF16)

    jq, jk = Z_QA // PROJ_TN, Z_KA // PROJ_TN
    j_sig = Z_OM // PROJ_TN

    @pl.when(j == jq)
    def _():
        head_norm(qg_ref, HEAD_DIM_A ** -0.5)

    @pl.when(j == jk)
    def _():
        head_norm(kg_ref, 1.0)

    @pl.when(jnp.logical_and(j > jk, j < j_sig))
    def _():
        z_ref[...] = acc.astype(BF16)

    @pl.when(j >= j_sig)
    def _():
        z_ref[...] = jax.nn.sigmoid(acc).astype(BF16)


def _proj(x2, g, mod, w_p, w_if, b_if, q_g, k_g, layer, row0, seq):
    t, d = x2.shape
    tpb = seq // PROJ_TM
    return pl.pallas_call(
        functools.partial(_proj_kernel, row0=row0),
        out_shape=(jax.ShapeDtypeStruct((t, Z_COLS), BF16), jax.ShapeDtypeStruct((t, IF_COLS), F32)),
        grid=(t // PROJ_TM, Z_COLS // PROJ_TN),
        in_specs=[pl.BlockSpec((PROJ_TM, d), lambda i, j: (i, 0)),
                  pl.BlockSpec((1, d), lambda i, j: (0, 0)),
                  pl.BlockSpec((None, 9, d), lambda i, j: (i // tpb, 0, 0)),
                  pl.BlockSpec((None, d, PROJ_TN), lambda i, j: (layer, 0, j)),
                  pl.BlockSpec((None, d, IF_COLS), lambda i, j: (layer, 0, 0)),
                  pl.BlockSpec((1, IF_COLS), lambda i, j: (0, 0)),
                  pl.BlockSpec((1, HEAD_DIM_A), lambda i, j: (0, 0)),
                  pl.BlockSpec((1, HEAD_DIM_A), lambda i, j: (0, 0))],
        out_specs=(pl.BlockSpec((PROJ_TM, PROJ_TN), lambda i, j: (i, j)),
                   pl.BlockSpec((PROJ_TM, IF_COLS), lambda i, j: (i, 0))),
        scratch_shapes=[pltpu.VMEM((PROJ_TM, d), BF16)],
        compiler_params=_params(("parallel", "arbitrary")),
        name="proj",
    )(x2, g, mod, w_p, w_if, b_if, q_g, k_g)


def _attn_bias(table):
    r = np.arange(ATT_TQ)[:, None]
    c = np.arange(ATT_W)[None, :]
    out = []
    for v in range(LEFT // ATT_TQ + 1):
        shift = v * ATT_TQ
        idx = np.clip(shift + r - c, -(CHUNK - 1), MAX_REL) + (CHUNK - 1)
        dq = (shift + r) // CHUNK - c // CHUNK
        valid = (dq >= 0) & (dq <= LEFT_CHUNKS)
        out.append(jnp.where(valid[None], table[:, idx], NEG))
    return jnp.stack(out, axis=1)


def _attn_kernel(q_ref, k_ref, v_ref, bias_ref, o_ref):
    qi = pl.program_id(2)
    var = jnp.minimum(qi, LEFT // ATT_TQ)
    ks = pl.multiple_of((qi - var) * ATT_TQ, ATT_TQ)
    for hd in range(ATT_HG):
        cols = slice(hd * HEAD_DIM_A, (hd + 1) * HEAD_DIM_A)
        q = q_ref[:, cols]
        k = k_ref[pl.ds(ks, ATT_W), cols]
        v = v_ref[pl.ds(ks, ATT_W), cols]
        s = lax.dot_general(q, k, (((1,), (1,)), ((), ())), preferred_element_type=F32)
        s = s + bias_ref[hd, var]
        p = jnp.exp(s - jnp.max(s, axis=-1, keepdims=True))
        l = jnp.sum(p, axis=-1, keepdims=True)
        o = jnp.dot(p.astype(BF16), v, preferred_element_type=F32)
        o_ref[:, cols] = (o / l).astype(BF16)


def _attention(z3, bias):
    b, s, _ = z3.shape
    gw = ATT_HG * HEAD_DIM_A
    nvar = bias.shape[1]
    return pl.pallas_call(
        _attn_kernel,
        out_shape=jax.ShapeDtypeStruct((b, s, D_ATTN), BF16),
        grid=(b, N_HEADS_A // ATT_HG, s // ATT_TQ),
        in_specs=[pl.BlockSpec((None, ATT_TQ, gw), lambda bi, g, qi: (bi, qi, Z_QA // gw + g)),
                  pl.BlockSpec((None, s, gw), lambda bi, g, qi: (bi, 0, Z_KA // gw + g)),
                  pl.BlockSpec((None, s, gw), lambda bi, g, qi: (bi, 0, Z_VA // gw + g)),
                  pl.BlockSpec((ATT_HG, nvar, ATT_TQ, ATT_W), lambda bi, g, qi: (g, 0, 0, 0))],
        out_specs=pl.BlockSpec((None, ATT_TQ, gw), lambda bi, g, qi: (bi, qi, g)),
        compiler_params=_params(("parallel", "parallel", "arbitrary")),
        name="attn",
    )(z3, z3, z3, bias)


def _mlstm_kernel(uq_ref, uk_ref, v_ref, o_ref, zif_ref, cwq_ref, cwk_ref, cbq_ref, cbk_ref, g_ref, out_ref,
                  c_ref, n_ref, m_ref, eq_ref, ek_ref):
    L = ML_L
    ci = pl.program_id(1)

    @pl.when(ci == 0)
    def _():
        c_ref[...] = jnp.zeros_like(c_ref)
        n_ref[...] = jnp.zeros_like(n_ref)
        m_ref[...] = jnp.zeros_like(m_ref)
        eq_ref[0:8, :] = jnp.zeros((8, D_MLSTM), F32)
        ek_ref[0:8, :] = jnp.zeros((8, D_MLSTM), F32)

    def conv_silu(u_ref, e_ref, cw_ref, cb_ref):
        e_ref[8:8 + L, :] = u_ref[...].astype(F32)
        acc = cb_ref[...] + e_ref[8:8 + L, :] * cw_ref[CONV_W - 1:CONV_W, :]
        for dlt in range(1, CONV_W):
            acc = acc + e_ref[8 - dlt:8 - dlt + L, :] * cw_ref[CONV_W - 1 - dlt:CONV_W - dlt, :]
        e_ref[0:8, :] = e_ref[L:L + 8, :]
        return acc * jax.nn.sigmoid(acc)

    q_all = conv_silu(uq_ref, eq_ref, cwq_ref, cbq_ref)
    k_all = conv_silu(uk_ref, ek_ref, cwk_ref, cbk_ref) * (HEAD_DIM_M ** -0.5)

    zi = zif_ref[...]
    logf = jnp.minimum(zi, 0.0) - jnp.log1p(jnp.exp(-jnp.abs(zi)))
    row = lax.broadcasted_iota(jnp.int32, (L, L), 0)
    col = lax.broadcasted_iota(jnp.int32, (L, L), 1)
    causal = row >= col
    bcs = jnp.dot(causal.astype(F32), logf, preferred_element_type=F32,
                  precision=lax.Precision.HIGHEST)
    zi_t = zi.T
    bcs_t = bcs.T

    for hd in range(N_HEADS_M):
        cols = slice(hd * HEAD_DIM_M, (hd + 1) * HEAD_DIM_M)
        q = q_all[:, cols]
        k = k_all[:, cols]
        qb = q.astype(BF16)
        kb = k.astype(BF16)
        v = v_ref[:, cols]
        i_col = zi[:, hd:hd + 1]
        b_col = bcs[:, N_HEADS_M + hd:N_HEADS_M + hd + 1]
        i_row = zi_t[hd:hd + 1, :]
        b_row = bcs_t[N_HEADS_M + hd:N_HEADS_M + hd + 1, :]
        m_prev = m_ref[hd, 0:1, 0:1]
        c_prev = c_ref[hd]
        n_prev = n_ref[hd]

        d = jnp.where(causal, b_col + (i_row - b_row), NEG)
        m_inter = b_col + m_prev
        m_t = jnp.maximum(m_inter, jnp.max(d, axis=-1, keepdims=True))
        qk = lax.dot_general(qb, kb, (((1,), (1,)), ((), ())), preferred_element_type=F32)
        sm = jnp.exp(d - m_t) * qk
        w_inter = jnp.exp(m_inter - m_t)
        num = (jnp.dot(sm.astype(BF16), v, preferred_element_type=F32)
               + w_inter * jnp.dot(qb, c_prev.astype(BF16), preferred_element_type=F32))
        den = (jnp.sum(sm, axis=-1, keepdims=True)
               + w_inter * jnp.sum(q * n_prev, axis=-1, keepdims=True))
        hh = num / jnp.maximum(jnp.abs(den), jnp.exp(-m_t))

        b_last = b_col[L - 1:L, :]
        w_end = b_last - b_col + i_col
        g_end = jnp.max(w_end, axis=0, keepdims=True)
        m_new = jnp.maximum(b_last + m_prev, g_end)
        decay = jnp.exp(b_last + m_prev - m_new)
        inj = jnp.exp(g_end - m_new)
        kw = k * jnp.exp(w_end - g_end)
        a_c = lax.dot_general(kw.astype(BF16), v, (((0,), (0,)), ((), ())), preferred_element_type=F32)
        c_ref[hd] = decay * c_prev + inj * a_c
        n_ref[hd] = decay * n_prev + inj * jnp.sum(kw, axis=0, keepdims=True)
        m_ref[hd] = jnp.broadcast_to(m_new, m_ref.shape[1:])

        hn = hh * lax.rsqrt(jnp.mean(hh * hh, axis=-1, keepdims=True) + EPS) * g_ref[:, cols]
        out_ref[:, cols] = (o_ref[:, cols].astype(F32) * hn).astype(BF16)


def _mlstm(z3, zif3, conv_w, conv_b, m_g):
    b, s, _ = z3.shape
    L = ML_L
    dm = D_MLSTM
    return pl.pallas_call(
        _mlstm_kernel,
        out_shape=jax.ShapeDtypeStruct((b, s, dm), BF16),
        grid=(b, s // L),
        in_specs=[pl.BlockSpec((None, L, dm), lambda bi, ci: (bi, ci, Z_QM // dm)),
                  pl.BlockSpec((None, L, dm), lambda bi, ci: (bi, ci, Z_KM // dm)),
                  pl.BlockSpec((None, L, dm), lambda bi, ci: (bi, ci, Z_VM // dm)),
                  pl.BlockSpec((None, L, dm), lambda bi, ci: (bi, ci, Z_OM // dm)),
                  pl.BlockSpec((None, L, IF_COLS), lambda bi, ci: (bi, ci, 0)),
                  pl.BlockSpec((CONV_W, dm), lambda bi, ci: (0, 0)),
                  pl.BlockSpec((CONV_W, dm), lambda bi, ci: (0, 1)),
                  pl.BlockSpec((1, dm), lambda bi, ci: (0, 0)),
                  pl.BlockSpec((1, dm), lambda bi, ci: (0, 1)),
                  pl.BlockSpec((1, dm), lambda bi, ci: (0, 0))],
        out_specs=pl.BlockSpec((None, L, dm), lambda bi, ci: (bi, ci, 0)),
        scratch_shapes=[pltpu.VMEM((N_HEADS_M, HEAD_DIM_M, HEAD_DIM_M), F32),
                        pltpu.VMEM((N_HEADS_M, 1, HEAD_DIM_M), F32),
                        pltpu.VMEM((N_HEADS_M, 8, 128), F32),
                        pltpu.VMEM((L + 8, dm), F32),
                        pltpu.VMEM((L + 8, dm), F32)],
        compiler_params=_params(("parallel", "arbitrary")),
        name="mlstm",
    )(z3, z3, z3, z3, zif3, conv_w, conv_w, conv_b, conv_b, m_g)


def _merge_kernel(x_ref, mod_ref, a_ref, hm_ref, ga_ref, gm_ref, wa_ref, wm_ref, wo_ref, o_ref, acc_ref, *, row0):
    j = pl.program_id(1)

    @pl.when(j == 0)
    def _():
        acc_ref[...] = jnp.zeros_like(acc_ref)

    ua = jnp.dot(a_ref[...], wa_ref[...], preferred_element_type=F32)
    um = jnp.dot(hm_ref[...], wm_ref[...], preferred_element_type=F32)
    merged = ga_ref[...].astype(F32) * ua + gm_ref[...].astype(F32) * um
    acc_ref[...] += jnp.dot(merged.astype(BF16), wo_ref[...], preferred_element_type=F32)

    @pl.when(j == pl.num_programs(1) - 1)
    def _():
        o_ref[...] = x_ref[...] + mod_ref[row0 + 2:row0 + 3, :] * acc_ref[...]


def _merge(x2, mod, attn2, hm2, z2, w_up_a, w_up_m, w_out, layer, row0, seq):
    t, d = x2.shape
    tpb = seq // MERGE_TM
    tn = MERGE_TN
    return pl.pallas_call(
        functools.partial(_merge_kernel, row0=row0),
        out_shape=jax.ShapeDtypeStruct((t, d), F32),
        grid=(t // MERGE_TM, d // tn),
        in_specs=[pl.BlockSpec((MERGE_TM, d), lambda i, j: (i, 0)),
                  pl.BlockSpec((None, 9, d), lambda i, j: (i // tpb, 0, 0)),
                  pl.BlockSpec((MERGE_TM, D_ATTN), lambda i, j: (i, 0)),
                  pl.BlockSpec((MERGE_TM, D_MLSTM), lambda i, j: (i, 0)),
                  pl.BlockSpec((MERGE_TM, tn), lambda i, j: (i, Z_GA // tn + j)),
                  pl.BlockSpec((MERGE_TM, tn), lambda i, j: (i, Z_GM // tn + j)),
                  pl.BlockSpec((None, D_ATTN, tn), lambda i, j: (layer, 0, j)),
                  pl.BlockSpec((None, D_MLSTM, tn), lambda i, j: (layer, 0, j)),
                  pl.BlockSpec((None, tn, d), lambda i, j: (layer, j, 0))],
        out_specs=pl.BlockSpec((MERGE_TM, d), lambda i, j: (i, 0)),
        scratch_shapes=[pltpu.VMEM((MERGE_TM, d), F32)],
        compiler_params=_params(("parallel", "arbitrary")),
        name="merge",
    )(x2, mod, attn2, hm2, z2, z2, w_up_a, w_up_m, w_out)


def kernel(x, c, norm_g, w_ada, b_ada, ffn1_w1, ffn1_w3, ffn1_w2, w_in, b_if, conv_w, conv_b, q_norm_g, k_norm_g,
           rel_table, m_norm_g, w_up_a, w_up_m, w_out, ffn2_w1, ffn2_w3, ffn2_w2):
    bsz, seq, d = x.shape
    depth = w_ada.shape[0]
    t = bsz * seq
    assert d == D_MODEL and seq % FFN_TM == 0 and seq % ML_L == 0 and seq >= ATT_W and bsz <= 8

    c8 = jnp.zeros((8, d), F32).at[:bsz].set(c)
    mod_all = _ada(c8, w_ada, b_ada)[:, :bsz].reshape(depth, bsz, 9, d)

    f1 = (_cast_pad(ffn1_w1, d, D_FF_PAD, d, CAST_B), _cast_pad(ffn1_w3, d, D_FF_PAD, d, CAST_B),
          _cast_pad(ffn1_w2, D_FF_PAD, d, CAST_B, d))
    f2 = (_cast_pad(ffn2_w1, d, D_FF_PAD, d, CAST_B), _cast_pad(ffn2_w3, d, D_FF_PAD, d, CAST_B),
          _cast_pad(ffn2_w2, D_FF_PAD, d, CAST_B, d))
    w_p, w_if = _cast_win(w_in)
    wa = _cast_pad(w_up_a, D_ATTN, d, D_ATTN, d)
    wm = _cast_pad(w_up_m, D_MLSTM, d, D_MLSTM, d)
    wo = _cast_pad(w_out, d, d, d // 2, d)

    x2 = x.reshape(t, d)
    for l in range(depth):
        mod = mod_all[l]
        x2 = _ffn(x2, norm_g[l, 0:1], mod, *f1, l, 0, seq)

        bif = jnp.pad(b_if[l], (0, IF_COLS - N_IF)).reshape(1, IF_COLS)
        z2, zif2 = _proj(x2, norm_g[l, 1:2], mod, w_p, w_if, bif, q_norm_g[l].reshape(1, -1),
                         k_norm_g[l].reshape(1, -1), l, 3, seq)
        z3 = z2.reshape(bsz, seq, Z_COLS)
        attn = _attention(z3, _attn_bias(rel_table[l]))
        hm = _mlstm(z3, zif2.reshape(bsz, seq, IF_COLS), conv_w[l], conv_b[l].reshape(1, -1),
                    m_norm_g[l].reshape(1, -1))
        x2 = _merge(x2, mod, attn.reshape(t, D_ATTN), hm.reshape(t, D_MLSTM), z2, wa, wm, wo, l, 3, seq)

        x2 = _ffn(x2, norm_g[l, 2:3], mod, *f2, l, 6, seq)
    return x2.reshape(bsz, seq, d)
```

```python
import functools

import numpy as np
import jax
import jax.numpy as jnp
from jax import lax
from jax.experimental import pallas as pl
from jax.experimental.pallas import tpu as pltpu

F32 = jnp.float32
BF16 = jnp.bfloat16

D_MODEL = 2048
CHUNK = 64
N_HEADS_A = 8
HEAD_DIM_A = 128
D_ATTN = N_HEADS_A * HEAD_DIM_A
LEFT_CHUNKS = 8
LEFT = LEFT_CHUNKS * CHUNK
MAX_REL = 4 * CHUNK
N_HEADS_M = 4
HEAD_DIM_M = 256
D_MLSTM = N_HEADS_M * HEAD_DIM_M
CONV_W = 4
D_FF = ((8 * D_MODEL // 3 + 127) // 128) * 128
EPS = 1e-6
NEG = -1e30

SIG_GA, SIG_GM, SIG_OM = 0, D_MODEL, 2 * D_MODEL
SIG_COLS = D_MLSTM + 2 * D_MODEL
NORM_QA, NORM_KA = 0, D_ATTN
NORM_COLS = 2 * D_ATTN
PLAIN_VA, PLAIN_QM, PLAIN_KM, PLAIN_VM = 0, D_ATTN, D_ATTN + D_MLSTM, D_ATTN + 2 * D_MLSTM
PLAIN_COLS = D_ATTN + 3 * D_MLSTM
WP_SIG, WP_NORM, WP_PLAIN = 0, SIG_COLS, SIG_COLS + NORM_COLS
Z_COLS = SIG_COLS + NORM_COLS + PLAIN_COLS
IF_COLS = 128
N_IF = 2 * N_HEADS_M
WIN_OM = 3 * D_ATTN + 3 * D_MLSTM
IF0 = WIN_OM + D_MLSTM

FFN_TM, FFN_TF = 512, 512
D_FF_PAD = ((D_FF + FFN_TF - 1) // FFN_TF) * FFN_TF
PROJ_TM = 512
PROJ_STEPS = 4
PROJ_SIG_W, PROJ_NORM_W, PROJ_PLAIN_W = SIG_COLS // PROJ_STEPS, NORM_COLS // PROJ_STEPS, PLAIN_COLS // PROJ_STEPS
ATT_TQ = 256
ATT_W = LEFT + ATT_TQ
ATT_NVAR = LEFT // ATT_TQ + 1
ATT_ROLL = 1024
ATT_HG = 4
CHUNK_LOG2 = 6
assert ATT_W + ATT_TQ - 1 <= ATT_ROLL and 1 << CHUNK_LOG2 == CHUNK
ML_L = 256
MERGE_TM, MERGE_TN = 512, 512
ADA_TN = 1024
WIN_TN = 1024
WIN_J_GATE0, WIN_J_GATE1 = SIG_GA // WIN_TN, SIG_OM // WIN_TN
CAST_B = 512
MIB = 1024 * 1024
VMEM_LIMIT = 48 * MIB
VMEM_LIMIT_MERGE = 58 * MIB


def _params(sem, vmem_limit=VMEM_LIMIT):
    return pltpu.CompilerParams(dimension_semantics=sem, vmem_limit_bytes=vmem_limit)


def _modulated(x, g, mod_ref, row0):
    y = x * lax.rsqrt(jnp.mean(x * x, axis=-1, keepdims=True) + EPS)
    return (y * g) * (1.0 + mod_ref[row0 + 1:row0 + 2, :]) + mod_ref[row0:row0 + 1, :]


def _ada_kernel(c_ref, w_ref, b_ref, o_ref):
    c = c_ref[...]
    ca = (c * jax.nn.sigmoid(c)).astype(BF16)
    o_ref[...] = jnp.dot(ca, w_ref[...].astype(BF16), preferred_element_type=F32) + b_ref[...]


def _ada(c8, w_ada, b_ada):
    depth, d, n = w_ada.shape
    return pl.pallas_call(
        _ada_kernel,
        out_shape=jax.ShapeDtypeStruct((depth, 8, n), F32),
        grid=(depth, n // ADA_TN),
        in_specs=[pl.BlockSpec((8, d), lambda l, j: (0, 0)),
                  pl.BlockSpec((None, d, ADA_TN), lambda l, j: (l, 0, j)),
                  pl.BlockSpec((None, 1, ADA_TN), lambda l, j: (l, 0, j))],
        out_specs=pl.BlockSpec((None, 8, ADA_TN), lambda l, j: (l, 0, j)),
        compiler_params=_params(("parallel", "parallel")),
        name="adaln",
    )(c8, w_ada, b_ada.reshape(depth, 1, n))


def _cast_kernel(w_ref, o_ref, *, rows, cols):
    br, bc = o_ref.shape
    x = w_ref[...]
    if rows % br or cols % bc:
        r = lax.broadcasted_iota(jnp.int32, (br, bc), 0) + pl.program_id(1) * br
        c = lax.broadcasted_iota(jnp.int32, (br, bc), 1) + pl.program_id(2) * bc
        x = jnp.where(jnp.logical_and(r < rows, c < cols), x, 0.0)
    o_ref[...] = x.astype(BF16)


def _cast_pad(w, rows_out, cols_out, br, bc):
    depth, rows, cols = w.shape
    return pl.pallas_call(
        functools.partial(_cast_kernel, rows=rows, cols=cols),
        out_shape=jax.ShapeDtypeStruct((depth, rows_out, cols_out), BF16),
        grid=(depth, rows_out // br, cols_out // bc),
        in_specs=[pl.BlockSpec((None, br, bc), lambda l, i, j: (l, i, j))],
        out_specs=pl.BlockSpec((None, br, bc), lambda l, i, j: (l, i, j)),
        compiler_params=_params(("parallel", "parallel", "parallel")),
        name="cast",
    )(w)


def _cast_win_kernel(a_ref, b_ref, wp_ref, wif_ref):
    j = pl.program_id(1)
    gates = jnp.logical_and(j >= WIN_J_GATE0, j < WIN_J_GATE1)

    @pl.when(jnp.logical_not(gates))
    def _():
        wp_ref[...] = a_ref[...].astype(BF16)

    @pl.when(gates)
    def _():
        full = jnp.concatenate([a_ref[...], b_ref[...]], axis=1)
        wp_ref[...] = pltpu.roll(full, full.shape[1] - N_IF, axis=1)[:, :WIN_TN].astype(BF16)

    @pl.when(j == WIN_J_GATE0)
    def _():
        lane = lax.broadcasted_iota(jnp.int32, wif_ref.shape, 1)
        wif_ref[...] = jnp.where(lane < N_IF, a_ref[:, :IF_COLS], 0.0).astype(BF16)


def _win_block(j):
    j_om = WIN_OM // WIN_TN
    return jnp.where(j < WIN_J_GATE1, j + j_om + 1, jnp.where(j == WIN_J_GATE1, j_om, j - (WIN_J_GATE1 + 1)))


def _cast_win(w_in):
    depth, d, _ = w_in.shape
    per = WIN_TN // IF_COLS

    def next_block(l, j):
        jg = jnp.clip(j, WIN_J_GATE0, WIN_J_GATE1 - 1)
        return (l, 0, (_win_block(jg) + 1) * per)

    return pl.pallas_call(
        _cast_win_kernel,
        out_shape=(jax.ShapeDtypeStruct((depth, d, Z_COLS), BF16), jax.ShapeDtypeStruct((depth, d, IF_COLS), BF16)),
        grid=(depth, Z_COLS // WIN_TN),
        in_specs=[pl.BlockSpec((None, d, WIN_TN), lambda l, j: (l, 0, _win_block(j))),
                  pl.BlockSpec((None, d, IF_COLS), next_block)],
        out_specs=(pl.BlockSpec((None, d, WIN_TN), lambda l, j: (l, 0, j)),
                   pl.BlockSpec((None, d, IF_COLS), lambda l, j: (l, 0, 0))),
        compiler_params=_params(("parallel", "arbitrary")),
        name="cast_win",
    )(w_in, w_in)


def _ffn_kernel(x_ref, g_ref, mod_ref, w1_ref, w3_ref, w2_ref, o_ref, h_ref, acc_ref, *, row0):
    j = pl.program_id(1)

    @pl.when(j == 0)
    def _():
        h_ref[...] = _modulated(x_ref[...], g_ref[...], mod_ref, row0).astype(BF16)
        acc_ref[...] = jnp.zeros_like(acc_ref)

    h = h_ref[...]
    a = jnp.dot(h, w1_ref[...], preferred_element_type=F32)
    b = jnp.dot(h, w3_ref[...], preferred_element_type=F32)
    u = (a * jax.nn.sigmoid(a)) * b
    acc_ref[...] += jnp.dot(u.astype(BF16), w2_ref[...], preferred_element_type=F32)

    @pl.when(j == pl.num_programs(1) - 1)
    def _():
        o_ref[...] = x_ref[...] + (0.5 * mod_ref[row0 + 2:row0 + 3, :]) * acc_ref[...]


def _ffn(x2, g, mod, w1, w3, w2, layer, row0, seq):
    t, d = x2.shape
    tpb = seq // FFN_TM
    nf = w1.shape[2] // FFN_TF
    return pl.pallas_call(
        functools.partial(_ffn_kernel, row0=row0),
        out_shape=jax.ShapeDtypeStruct((t, d), F32),
        grid=(t // FFN_TM, nf),
        in_specs=[pl.BlockSpec((FFN_TM, d), lambda i, j: (i, 0)),
                  pl.BlockSpec((1, d), lambda i, j: (0, 0)),
                  pl.BlockSpec((None, 9, d), lambda i, j: (i // tpb, 0, 0)),
                  pl.BlockSpec((None, d, FFN_TF), lambda i, j: (layer, 0, j)),
                  pl.BlockSpec((None, d, FFN_TF), lambda i, j: (layer, 0, j)),
                  pl.BlockSpec((None, FFN_TF, d), lambda i, j: (layer, j, 0))],
        out_specs=pl.BlockSpec((FFN_TM, d), lambda i, j: (i, 0)),
        scratch_shapes=[pltpu.VMEM((FFN_TM, d), BF16), pltpu.VMEM((FFN_TM, d), F32)],
        compiler_params=_params(("parallel", "arbitrary")),
        name="ffn",
    )(x2, g, mod, w1, w3, w2)


def _proj_kernel(x_ref, g_ref, mod_ref, ws_ref, wn_ref, wp_ref, wif_ref, bif_ref, gain_ref,
                 sig_ref, norm_ref, plain_ref, zif_ref, h_ref, *, row0):
    @pl.when(pl.program_id(1) == 0)
    def _():
        hb = _modulated(x_ref[...], g_ref[...], mod_ref, row0).astype(BF16)
        h_ref[...] = hb
        zif_ref[...] = jnp.dot(hb, wif_ref[...], preferred_element_type=F32) + bif_ref[...]

    h = h_ref[...]
    sig_ref[...] = jax.nn.sigmoid(jnp.dot(h, ws_ref[...], preferred_element_type=F32)).astype(BF16)
    qk = jnp.dot(h, wn_ref[...], preferred_element_type=F32)
    for hd in range(PROJ_NORM_W // HEAD_DIM_A):
        cols = slice(hd * HEAD_DIM_A, (hd + 1) * HEAD_DIM_A)
        a = qk[:, cols]
        y = a * lax.rsqrt(jnp.mean(a * a, axis=-1, keepdims=True) + EPS)
        norm_ref[:, cols] = (y * gain_ref[...]).astype(BF16)
    plain_ref[...] = jnp.dot(h, wp_ref[...], preferred_element_type=F32).astype(BF16)


def _proj(x2, g, mod, w_p, w_if, b_if, qk_gain, layer, row0, seq):
    t, d = x2.shape
    tpb = seq // PROJ_TM
    sw, nw, pw = PROJ_SIG_W, PROJ_NORM_W, PROJ_PLAIN_W
    q_steps = D_ATTN // nw
    return pl.pallas_call(
        functools.partial(_proj_kernel, row0=row0),
        out_shape=(jax.ShapeDtypeStruct((t, SIG_COLS), BF16), jax.ShapeDtypeStruct((t, NORM_COLS), BF16),
                   jax.ShapeDtypeStruct((t, PLAIN_COLS), BF16), jax.ShapeDtypeStruct((t, IF_COLS), F32)),
        grid=(t // PROJ_TM, PROJ_STEPS),
        in_specs=[pl.BlockSpec((PROJ_TM, d), lambda i, j: (i, 0)),
                  pl.BlockSpec((1, d), lambda i, j: (0, 0)),
                  pl.BlockSpec((None, 9, d), lambda i, j: (i // tpb, 0, 0)),
                  pl.BlockSpec((None, d, sw), lambda i, j: (layer, 0, WP_SIG // sw + j)),
                  pl.BlockSpec((None, d, nw), lambda i, j: (layer, 0, WP_NORM // nw + j)),
                  pl.BlockSpec((None, d, pw), lambda i, j: (layer, 0, WP_PLAIN // pw + j)),
                  pl.BlockSpec((None, d, IF_COLS), lambda i, j: (layer, 0, 0)),
                  pl.BlockSpec((1, IF_COLS), lambda i, j: (0, 0)),
                  pl.BlockSpec((None, 1, HEAD_DIM_A), lambda i, j: (j // q_steps, 0, 0))],
        out_specs=(pl.BlockSpec((PROJ_TM, sw), lambda i, j: (i, j)),
                   pl.BlockSpec((PROJ_TM, nw), lambda i, j: (i, j)),
                   pl.BlockSpec((PROJ_TM, pw), lambda i, j: (i, j)),
                   pl.BlockSpec((PROJ_TM, IF_COLS), lambda i, j: (i, 0))),
        scratch_shapes=[pltpu.VMEM((PROJ_TM, d), BF16)],
        compiler_params=_params(("parallel", "arbitrary")),
        name="proj",
    )(x2, g, mod, w_p, w_p, w_p, w_if, b_if, qk_gain)


def _bias_kernel(g_ref, o_ref):
    r = lax.broadcasted_iota(jnp.int32, (ATT_TQ, ATT_W), 0)
    c = lax.broadcasted_iota(jnp.int32, (ATT_TQ, ATT_W), 1)
    for v in range(ATT_NVAR):
        x = jnp.broadcast_to(g_ref[v:v + 1, :], (ATT_TQ, ATT_ROLL))
        y = pltpu.roll(x, 0, 1, stride=1, stride_axis=0)[:, :ATT_W]
        dq = jnp.right_shift(r + v * ATT_TQ, CHUNK_LOG2) - jnp.right_shift(c, CHUNK_LOG2)
        valid = jnp.logical_and(dq >= 0, dq <= LEFT_CHUNKS)
        o_ref[v] = jnp.where(valid, y, NEG)


def _attn_bias(table):
    h = table.shape[0]
    p = np.arange(ATT_ROLL)
    u = np.where(p < ATT_W, p, p - ATT_ROLL)
    idx = np.stack([np.clip(v * ATT_TQ - u, -(CHUNK - 1), MAX_REL) + (CHUNK - 1) for v in range(ATT_NVAR)])
    g = jnp.take(table, jnp.asarray(idx.reshape(-1), jnp.int32), axis=1).reshape(h, ATT_NVAR, ATT_ROLL)
    return pl.pallas_call(
        _bias_kernel,
        out_shape=jax.ShapeDtypeStruct((h, ATT_NVAR, ATT_TQ, ATT_W), F32),
        grid=(h,),
        in_specs=[pl.BlockSpec((None, ATT_NVAR, ATT_ROLL), lambda i: (i, 0, 0))],
        out_specs=pl.BlockSpec((None, ATT_NVAR, ATT_TQ, ATT_W), lambda i: (i, 0, 0, 0)),
        compiler_params=_params(("parallel",)),
        name="attn_bias",
    )(g)


def _attn_kernel(q_ref, k_ref, v_ref, bias_ref, o_ref):
    qi = pl.program_id(2)
    var = jnp.minimum(qi, ATT_NVAR - 1)
    ks = pl.multiple_of((qi - var) * ATT_TQ, ATT_TQ)
    for hd in range(ATT_HG):
        cols = slice(hd * HEAD_DIM_A, (hd + 1) * HEAD_DIM_A)
        q = q_ref[:, cols]
        k = k_ref[pl.ds(ks, ATT_W), cols]
        v = v_ref[pl.ds(ks, ATT_W), cols]
        s = lax.dot_general(q, k, (((1,), (1,)), ((), ())), preferred_element_type=F32)
        s = s + bias_ref[hd, var]
        p = jnp.exp(s - jnp.max(s, axis=-1, keepdims=True))
        l = jnp.sum(p, axis=-1, keepdims=True)
        o = jnp.dot(p.astype(BF16), v, preferred_element_type=F32)
        o_ref[:, cols] = (o / l).astype(BF16)


def _attention(norm3, plain3, bias):
    b, s, _ = norm3.shape
    gw = ATT_HG * HEAD_DIM_A
    nvar = bias.shape[1]
    return pl.pallas_call(
        _attn_kernel,
        out_shape=jax.ShapeDtypeStruct((b, s, D_ATTN), BF16),
        grid=(b, N_HEADS_A // ATT_HG, s // ATT_TQ),
        in_specs=[pl.BlockSpec((None, ATT_TQ, gw), lambda bi, g, qi: (bi, qi, NORM_QA // gw + g)),
                  pl.BlockSpec((None, s, gw), lambda bi, g, qi: (bi, 0, NORM_KA // gw + g)),
                  pl.BlockSpec((None, s, gw), lambda bi, g, qi: (bi, 0, PLAIN_VA // gw + g)),
                  pl.BlockSpec((ATT_HG, nvar, ATT_TQ, ATT_W), lambda bi, g, qi: (g, 0, 0, 0))],
        out_specs=pl.BlockSpec((None, ATT_TQ, gw), lambda bi, g, qi: (bi, qi, g)),
        compiler_params=_params(("parallel", "parallel", "arbitrary")),
        name="attn",
    )(norm3, norm3, plain3, bias)


def _mlstm_kernel(uq_ref, uk_ref, v_ref, o_ref, zif_ref, cwq_ref, cwk_ref, cbq_ref, cbk_ref, g_ref, out_ref,
                  c_ref, n_ref, m_ref, eq_ref, ek_ref):
    L = ML_L
    ci = pl.program_id(1)

    @pl.when(ci == 0)
    def _():
        c_ref[...] = jnp.zeros_like(c_ref)
        n_ref[...] = jnp.zeros_like(n_ref)
        m_ref[...] = jnp.zeros_like(m_ref)
        eq_ref[0:8, :] = jnp.zeros((8, D_MLSTM), F32)
        ek_ref[0:8, :] = jnp.zeros((8, D_MLSTM), F32)

    def conv_silu(u_ref, e_ref, cw_ref, cb_ref):
        e_ref[8:8 + L, :] = u_ref[...].astype(F32)
        acc = cb_ref[...] + e_ref[8:8 + L, :] * cw_ref[CONV_W - 1:CONV_W, :]
        for dlt in range(1, CONV_W):
            acc = acc + e_ref[8 - dlt:8 - dlt + L, :] * cw_ref[CONV_W - 1 - dlt:CONV_W - dlt, :]
        e_ref[0:8, :] = e_ref[L:L + 8, :]
        return acc * jax.nn.sigmoid(acc)

    q_all = conv_silu(uq_ref, eq_ref, cwq_ref, cbq_ref)
    k_all = conv_silu(uk_ref, ek_ref, cwk_ref, cbk_ref) * (HEAD_DIM_M ** -0.5)

    zi = zif_ref[...]
    logf = jnp.minimum(zi, 0.0) - jnp.log1p(jnp.exp(-jnp.abs(zi)))
    row = lax.broadcasted_iota(jnp.int32, (L, L), 0)
    col = lax.broadcasted_iota(jnp.int32, (L, L), 1)
    causal = row >= col
    bcs = jnp.dot(causal.astype(F32), logf, preferred_element_type=F32,
                  precision=lax.Precision.HIGHEST)
    zi_t = zi.T
    bcs_t = bcs.T

    for hd in range(N_HEADS_M):
        cols = slice(hd * HEAD_DIM_M, (hd + 1) * HEAD_DIM_M)
        q = q_all[:, cols]
        k = k_all[:, cols]
        qb = q.astype(BF16)
        kb = k.astype(BF16)
        v = v_ref[:, cols]
        i_col = zi[:, hd:hd + 1]
        b_col = bcs[:, N_HEADS_M + hd:N_HEADS_M + hd + 1]
        i_row = zi_t[hd:hd + 1, :]
        b_row = bcs_t[N_HEADS_M + hd:N_HEADS_M + hd + 1, :]
        m_prev = m_ref[hd, 0:1, 0:1]
        c_prev = c_ref[hd]
        n_prev = n_ref[hd]

        d = jnp.where(causal, b_col + (i_row - b_row), NEG)
        m_inter = b_col + m_prev
        m_t = jnp.maximum(m_inter, jnp.max(d, axis=-1, keepdims=True))
        qk = lax.dot_general(qb, kb, (((1,), (1,)), ((), ())), preferred_element_type=F32)
        sm = jnp.exp(d - m_t) * qk
        w_inter = jnp.exp(m_inter - m_t)
        num = (jnp.dot(sm.astype(BF16), v, preferred_element_type=F32)
               + w_inter * jnp.dot(qb, c_prev.astype(BF16), preferred_element_type=F32))
        den = (jnp.sum(sm, axis=-1, keepdims=True)
               + w_inter * jnp.sum(q * n_prev, axis=-1, keepdims=True))
        hh = num / jnp.maximum(jnp.abs(den), jnp.exp(-m_t))

        b_last = b_col[L - 1:L, :]
        w_end = b_last - b_col + i_col
        g_end = jnp.max(w_end, axis=0, keepdims=True)
        m_new = jnp.maximum(b_last + m_prev, g_end)
        decay = jnp.exp(b_last + m_prev - m_new)
        inj = jnp.exp(g_end - m_new)
        kw = k * jnp.exp(w_end - g_end)
        a_c = lax.dot_general(kw.astype(BF16), v, (((0,), (0,)), ((), ())), preferred_element_type=F32)
        c_ref[hd] = decay * c_prev + inj * a_c
        n_ref[hd] = decay * n_prev + inj * jnp.sum(kw, axis=0, keepdims=True)
        m_ref[hd] = jnp.broadcast_to(m_new, m_ref.shape[1:])

        hn = hh * lax.rsqrt(jnp.mean(hh * hh, axis=-1, keepdims=True) + EPS) * g_ref[:, cols]
        out_ref[:, cols] = (o_ref[:, cols].astype(F32) * hn).astype(BF16)


def _mlstm(plain3, sig3, zif3, conv_w, conv_b, m_g):
    b, s, _ = plain3.shape
    L = ML_L
    dm = D_MLSTM
    return pl.pallas_call(
        _mlstm_kernel,
        out_shape=jax.ShapeDtypeStruct((b, s, dm), BF16),
        grid=(b, s // L),
        in_specs=[pl.BlockSpec((None, L, dm), lambda bi, ci: (bi, ci, PLAIN_QM // dm)),
                  pl.BlockSpec((None, L, dm), lambda bi, ci: (bi, ci, PLAIN_KM // dm)),
                  pl.BlockSpec((None, L, dm), lambda bi, ci: (bi, ci, PLAIN_VM // dm)),
                  pl.BlockSpec((None, L, dm), lambda bi, ci: (bi, ci, SIG_OM // dm)),
                  pl.BlockSpec((None, L, IF_COLS), lambda bi, ci: (bi, ci, 0)),
                  pl.BlockSpec((CONV_W, dm), lambda bi, ci: (0, 0)),
                  pl.BlockSpec((CONV_W, dm), lambda bi, ci: (0, 1)),
                  pl.BlockSpec((1, dm), lambda bi, ci: (0, 0)),
                  pl.BlockSpec((1, dm), lambda bi, ci: (0, 1)),
                  pl.BlockSpec((1, dm), lambda bi, ci: (0, 0))],
        out_specs=pl.BlockSpec((None, L, dm), lambda bi, ci: (bi, ci, 0)),
        scratch_shapes=[pltpu.VMEM((N_HEADS_M, HEAD_DIM_M, HEAD_DIM_M), F32),
                        pltpu.VMEM((N_HEADS_M, 1, HEAD_DIM_M), F32),
                        pltpu.VMEM((N_HEADS_M, 8, 128), F32),
                        pltpu.VMEM((L + 8, dm), F32),
                        pltpu.VMEM((L + 8, dm), F32)],
        compiler_params=_params(("parallel", "arbitrary")),
        name="mlstm",
    )(plain3, plain3, plain3, sig3, zif3, conv_w, conv_w, conv_b, conv_b, m_g)


def _merge_kernel(x_ref, mod_ref, a_ref, hm_ref, ga_ref, gm_ref, wa_ref, wm_ref, wo_ref, o_ref, *, row0):
    a = a_ref[...]
    hm = hm_ref[...]
    acc = None
    for j in range(D_MODEL // MERGE_TN):
        cols = slice(j * MERGE_TN, (j + 1) * MERGE_TN)
        ua = jnp.dot(a, wa_ref[:, cols], preferred_element_type=F32)
        um = jnp.dot(hm, wm_ref[:, cols], preferred_element_type=F32)
        merged = ga_ref[:, cols].astype(F32) * ua + gm_ref[:, cols].astype(F32) * um
        part = jnp.dot(merged.astype(BF16), wo_ref[cols, :], preferred_element_type=F32)
        acc = part if acc is None else acc + part
    o_ref[...] = x_ref[...] + mod_ref[row0 + 2:row0 + 3, :] * acc


def _merge(x2, mod, attn2, hm2, sig2, w_up_a, w_up_m, w_out, layer, row0, seq):
    t, d = x2.shape
    tpb = seq // MERGE_TM
    resident = pl.Buffered(1)
    return pl.pallas_call(
        functools.partial(_merge_kernel, row0=row0),
        out_shape=jax.ShapeDtypeStruct((t, d), F32),
        grid=(t // MERGE_TM,),
        in_specs=[pl.BlockSpec((MERGE_TM, d), lambda i: (i, 0)),
                  pl.BlockSpec((None, 9, d), lambda i: (i // tpb, 0, 0)),
                  pl.BlockSpec((MERGE_TM, D_ATTN), lambda i: (i, 0)),
                  pl.BlockSpec((MERGE_TM, D_MLSTM), lambda i: (i, 0)),
                  pl.BlockSpec((MERGE_TM, d), lambda i: (i, SIG_GA // d)),
                  pl.BlockSpec((MERGE_TM, d), lambda i: (i, SIG_GM // d)),
                  pl.BlockSpec((None, D_ATTN, d), lambda i: (layer, 0, 0), pipeline_mode=resident),
                  pl.BlockSpec((None, D_MLSTM, d), lambda i: (layer, 0, 0), pipeline_mode=resident),
                  pl.BlockSpec((None, d, d), lambda i: (layer, 0, 0), pipeline_mode=resident)],
        out_specs=pl.BlockSpec((MERGE_TM, d), lambda i: (i, 0)),
        compiler_params=_params(("parallel",), VMEM_LIMIT_MERGE),
        name="merge",
    )(x2, mod, attn2, hm2, sig2, sig2, w_up_a, w_up_m, w_out)


def kernel(x, c, norm_g, w_ada, b_ada, ffn1_w1, ffn1_w3, ffn1_w2, w_in, b_if, conv_w, conv_b, q_norm_g, k_norm_g,
           rel_table, m_norm_g, w_up_a, w_up_m, w_out, ffn2_w1, ffn2_w3, ffn2_w2):
    bsz, seq, d = x.shape
    depth = w_ada.shape[0]
    t = bsz * seq
    assert d == D_MODEL and seq % FFN_TM == 0 and seq % ML_L == 0 and seq >= ATT_W and bsz <= 8

    c8 = jnp.zeros((8, d), F32).at[:bsz].set(c)
    mod_all = _ada(c8, w_ada, b_ada)[:, :bsz].reshape(depth, bsz, 9, d)

    f1 = (_cast_pad(ffn1_w1, d, D_FF_PAD, d, CAST_B), _cast_pad(ffn1_w3, d, D_FF_PAD, d, CAST_B),
          _cast_pad(ffn1_w2, D_FF_PAD, d, CAST_B, d))
    f2 = (_cast_pad(ffn2_w1, d, D_FF_PAD, d, CAST_B), _cast_pad(ffn2_w3, d, D_FF_PAD, d, CAST_B),
          _cast_pad(ffn2_w2, D_FF_PAD, d, CAST_B, d))
    w_p, w_if = _cast_win(w_in)
    wa = _cast_pad(w_up_a, D_ATTN, d, D_ATTN, d)
    wm = _cast_pad(w_up_m, D_MLSTM, d, D_MLSTM, d)
    wo = _cast_pad(w_out, d, d, d // 2, d)

    x2 = x.reshape(t, d)
    for l in range(depth):
        mod = mod_all[l]
        x2 = _ffn(x2, norm_g[l, 0:1], mod, *f1, l, 0, seq)

        bif = jnp.pad(b_if[l], (0, IF_COLS - N_IF)).reshape(1, IF_COLS)
        qk_gain = jnp.stack([q_norm_g[l] * HEAD_DIM_A ** -0.5, k_norm_g[l]]).reshape(2, 1, HEAD_DIM_A)
        sig2, norm2, plain2, zif2 = _proj(x2, norm_g[l, 1:2], mod, w_p, w_if, bif, qk_gain, l, 3, seq)
        plain3 = plain2.reshape(bsz, seq, PLAIN_COLS)
        attn = _attention(norm2.reshape(bsz, seq, NORM_COLS), plain3, _attn_bias(rel_table[l]))
        hm = _mlstm(plain3, sig2.reshape(bsz, seq, SIG_COLS), zif2.reshape(bsz, seq, IF_COLS), conv_w[l],
                    conv_b[l].reshape(1, -1), m_norm_g[l].reshape(1, -1))
        x2 = _merge(x2, mod, attn.reshape(t, D_ATTN), hm.reshape(t, D_MLSTM), sig2, wa, wm, wo, l, 3, seq)

        x2 = _ffn(x2, norm_g[l, 2:3], mod, *f2, l, 6, seq)
    return x2.reshape(bsz, seq, d)
```

```python
import functools

import numpy as np
import jax
import jax.numpy as jnp
from jax import lax
from jax.experimental import pallas as pl
from jax.experimental.pallas import tpu as pltpu

F32 = jnp.float32
BF16 = jnp.bfloat16

D_MODEL = 2048
CHUNK = 64
N_HEADS_A = 8
HEAD_DIM_A = 128
D_ATTN = N_HEADS_A * HEAD_DIM_A
LEFT_CHUNKS = 8
LEFT = LEFT_CHUNKS * CHUNK
MAX_REL = 4 * CHUNK
N_HEADS_M = 4
HEAD_DIM_M = 256
D_MLSTM = N_HEADS_M * HEAD_DIM_M
CONV_W = 4
D_FF = ((8 * D_MODEL // 3 + 127) // 128) * 128
EPS = 1e-6
NEG = -1e30

SIG_GA, SIG_GM, SIG_OM = 0, D_MODEL, 2 * D_MODEL
SIG_COLS = D_MLSTM + 2 * D_MODEL
NORM_QA, NORM_KA = 0, D_ATTN
NORM_COLS = 2 * D_ATTN
PLAIN_VA, PLAIN_QM, PLAIN_KM, PLAIN_VM = 0, D_ATTN, D_ATTN + D_MLSTM, D_ATTN + 2 * D_MLSTM
PLAIN_COLS = D_ATTN + 3 * D_MLSTM
WP_SIG, WP_NORM, WP_PLAIN = 0, SIG_COLS, SIG_COLS + NORM_COLS
Z_COLS = SIG_COLS + NORM_COLS + PLAIN_COLS
IF_COLS = 128
N_IF = 2 * N_HEADS_M
WIN_OM = 3 * D_ATTN + 3 * D_MLSTM
IF0 = WIN_OM + D_MLSTM

FFN_TM, FFN_TF = 512, 512
D_FF_PAD = ((D_FF + FFN_TF - 1) // FFN_TF) * FFN_TF
PROJ_TM = 512
PROJ_STEPS = 4
PROJ_SIG_W, PROJ_NORM_W, PROJ_PLAIN_W = SIG_COLS // PROJ_STEPS, NORM_COLS // PROJ_STEPS, PLAIN_COLS // PROJ_STEPS
ATT_TQ = 256
ATT_W = LEFT + ATT_TQ
ATT_NVAR = LEFT // ATT_TQ + 1
ATT_ROLL = 1024
ATT_HG = 4
CHUNK_LOG2 = 6
assert ATT_W + ATT_TQ - 1 <= ATT_ROLL and 1 << CHUNK_LOG2 == CHUNK
ML_L = 256
MERGE_TM, MERGE_TN = 512, 512
ADA_TN = 1024
WIN_TN = 1024
WIN_J_GATE0, WIN_J_GATE1 = SIG_GA // WIN_TN, SIG_OM // WIN_TN
CAST_B = 512
MIB = 1024 * 1024
VMEM_LIMIT = 48 * MIB
VMEM_LIMIT_MERGE = 58 * MIB


def _params(sem, vmem_limit=VMEM_LIMIT):
    return pltpu.CompilerParams(dimension_semantics=sem, vmem_limit_bytes=vmem_limit)


def _modulated(x, g, mod_ref, row0):
    y = x * lax.rsqrt(jnp.mean(x * x, axis=-1, keepdims=True) + EPS)
    return (y * g) * (1.0 + mod_ref[row0 + 1:row0 + 2, :]) + mod_ref[row0:row0 + 1, :]


def _ada_kernel(c_ref, w_ref, b_ref, o_ref):
    c = c_ref[...]
    ca = (c * jax.nn.sigmoid(c)).astype(BF16)
    o_ref[...] = jnp.dot(ca, w_ref[...].astype(BF16), preferred_element_type=F32) + b_ref[...]


def _ada(c8, w_ada, b_ada):
    depth, d, n = w_ada.shape
    return pl.pallas_call(
        _ada_kernel,
        out_shape=jax.ShapeDtypeStruct((depth, 8, n), F32),
        grid=(depth, n // ADA_TN),
        in_specs=[pl.BlockSpec((8, d), lambda l, j: (0, 0)),
                  pl.BlockSpec((None, d, ADA_TN), lambda l, j: (l, 0, j)),
                  pl.BlockSpec((None, 1, ADA_TN), lambda l, j: (l, 0, j))],
        out_specs=pl.BlockSpec((None, 8, ADA_TN), lambda l, j: (l, 0, j)),
        compiler_params=_params(("parallel", "parallel")),
        name="adaln",
    )(c8, w_ada, b_ada.reshape(depth, 1, n))


def _cast_kernel(w_ref, o_ref, *, rows, cols):
    br, bc = o_ref.shape
    x = w_ref[...]
    if rows % br or cols % bc:
        r = lax.broadcasted_iota(jnp.int32, (br, bc), 0) + pl.program_id(1) * br
        c = lax.broadcasted_iota(jnp.int32, (br, bc), 1) + pl.program_id(2) * bc
        x = jnp.where(jnp.logical_and(r < rows, c < cols), x, 0.0)
    o_ref[...] = x.astype(BF16)


def _cast_pad(w, rows_out, cols_out, br, bc):
    depth, rows, cols = w.shape
    return pl.pallas_call(
        functools.partial(_cast_kernel, rows=rows, cols=cols),
        out_shape=jax.ShapeDtypeStruct((depth, rows_out, cols_out), BF16),
        grid=(depth, rows_out // br, cols_out // bc),
        in_specs=[pl.BlockSpec((None, br, bc), lambda l, i, j: (l, i, j))],
        out_specs=pl.BlockSpec((None, br, bc), lambda l, i, j: (l, i, j)),
        compiler_params=_params(("parallel", "parallel", "parallel")),
        name="cast",
    )(w)


def _cast_win_kernel(a_ref, b_ref, wp_ref, wif_ref):
    j = pl.program_id(1)
    gates = jnp.logical_and(j >= WIN_J_GATE0, j < WIN_J_GATE1)

    @pl.when(jnp.logical_not(gates))
    def _():
        wp_ref[...] = a_ref[...].astype(BF16)

    @pl.when(gates)
    def _():
        full = jnp.concatenate([a_ref[...], b_ref[...]], axis=1)
        wp_ref[...] = pltpu.roll(full, full.shape[1] - N_IF, axis=1)[:, :WIN_TN].astype(BF16)

    @pl.when(j == WIN_J_GATE0)
    def _():
        lane = lax.broadcasted_iota(jnp.int32, wif_ref.shape, 1)
        wif_ref[...] = jnp.where(lane < N_IF, a_ref[:, :IF_COLS], 0.0).astype(BF16)


def _win_block(j):
    j_om = WIN_OM // WIN_TN
    return jnp.where(j < WIN_J_GATE1, j + j_om + 1, jnp.where(j == WIN_J_GATE1, j_om, j - (WIN_J_GATE1 + 1)))


def _cast_win(w_in):
    depth, d, _ = w_in.shape
    per = WIN_TN // IF_COLS

    def next_block(l, j):
        jg = jnp.clip(j, WIN_J_GATE0, WIN_J_GATE1 - 1)
        return (l, 0, (_win_block(jg) + 1) * per)

    return pl.pallas_call(
        _cast_win_kernel,
        out_shape=(jax.ShapeDtypeStruct((depth, d, Z_COLS), BF16), jax.ShapeDtypeStruct((depth, d, IF_COLS), BF16)),
        grid=(depth, Z_COLS // WIN_TN),
        in_specs=[pl.BlockSpec((None, d, WIN_TN), lambda l, j: (l, 0, _win_block(j))),
                  pl.BlockSpec((None, d, IF_COLS), next_block)],
        out_specs=(pl.BlockSpec((None, d, WIN_TN), lambda l, j: (l, 0, j)),
                   pl.BlockSpec((None, d, IF_COLS), lambda l, j: (l, 0, 0))),
        compiler_params=_params(("parallel", "arbitrary")),
        name="cast_win",
    )(w_in, w_in)


def _ffn_kernel(x_ref, g_ref, mod_ref, w1_ref, w3_ref, w2_ref, o_ref, h_ref, acc_ref, *, row0):
    j = pl.program_id(1)

    @pl.when(j == 0)
    def _():
        h_ref[...] = _modulated(x_ref[...], g_ref[...], mod_ref, row0).astype(BF16)

    h = h_ref[...]
    a = jnp.dot(h, w1_ref[...], preferred_element_type=F32)
    b = jnp.dot(h, w3_ref[...], preferred_element_type=F32)
    u = (a * jax.nn.sigmoid(a)) * b
    prev = jnp.where(j == 0, 0.0, acc_ref[...])
    acc_ref[...] = prev + jnp.dot(u.astype(BF16), w2_ref[...], preferred_element_type=F32)

    @pl.when(j == pl.num_programs(1) - 1)
    def _():
        o_ref[...] = x_ref[...] + (0.5 * mod_ref[row0 + 2:row0 + 3, :]) * acc_ref[...]


def _ffn(x2, g, mod, w1, w3, w2, layer, row0, seq):
    t, d = x2.shape
    tpb = seq // FFN_TM
    nf = w1.shape[2] // FFN_TF
    return pl.pallas_call(
        functools.partial(_ffn_kernel, row0=row0),
        out_shape=jax.ShapeDtypeStruct((t, d), F32),
        grid=(t // FFN_TM, nf),
        in_specs=[pl.BlockSpec((FFN_TM, d), lambda i, j: (i, 0)),
                  pl.BlockSpec((1, d), lambda i, j: (0, 0)),
                  pl.BlockSpec((None, 9, d), lambda i, j: (i // tpb, 0, 0)),
                  pl.BlockSpec((None, d, FFN_TF), lambda i, j: (layer, 0, j)),
                  pl.BlockSpec((None, d, FFN_TF), lambda i, j: (layer, 0, j)),
                  pl.BlockSpec((None, FFN_TF, d), lambda i, j: (layer, j, 0))],
        out_specs=pl.BlockSpec((FFN_TM, d), lambda i, j: (i, 0)),
        scratch_shapes=[pltpu.VMEM((FFN_TM, d), BF16), pltpu.VMEM((FFN_TM, d), F32)],
        compiler_params=_params(("parallel", "arbitrary")),
        name="ffn",
    )(x2, g, mod, w1, w3, w2)


def _proj_kernel(x_ref, g_ref, mod_ref, ws_ref, wn_ref, wp_ref, wif_ref, bif_ref, gain_ref,
                 sig_ref, norm_ref, plain_ref, zif_ref, h_ref, *, row0):
    @pl.when(pl.program_id(1) == 0)
    def _():
        hb = _modulated(x_ref[...], g_ref[...], mod_ref, row0).astype(BF16)
        h_ref[...] = hb
        zif_ref[...] = jnp.dot(hb, wif_ref[...], preferred_element_type=F32) + bif_ref[...]

    h = h_ref[...]
    sig_ref[...] = jax.nn.sigmoid(jnp.dot(h, ws_ref[...], preferred_element_type=F32)).astype(BF16)
    qk = jnp.dot(h, wn_ref[...], preferred_element_type=F32)
    for hd in range(PROJ_NORM_W // HEAD_DIM_A):
        cols = slice(hd * HEAD_DIM_A, (hd + 1) * HEAD_DIM_A)
        a = qk[:, cols]
        y = a * lax.rsqrt(jnp.mean(a * a, axis=-1, keepdims=True) + EPS)
        norm_ref[:, cols] = (y * gain_ref[...]).astype(BF16)
    plain_ref[...] = jnp.dot(h, wp_ref[...], preferred_element_type=F32).astype(BF16)


def _proj(x2, g, mod, w_p, w_if, b_if, qk_gain, layer, row0, seq):
    t, d = x2.shape
    tpb = seq // PROJ_TM
    sw, nw, pw = PROJ_SIG_W, PROJ_NORM_W, PROJ_PLAIN_W
    q_steps = D_ATTN // nw
    return pl.pallas_call(
        functools.partial(_proj_kernel, row0=row0),
        out_shape=(jax.ShapeDtypeStruct((t, SIG_COLS), BF16), jax.ShapeDtypeStruct((t, NORM_COLS), BF16),
                   jax.ShapeDtypeStruct((t, PLAIN_COLS), BF16), jax.ShapeDtypeStruct((t, IF_COLS), F32)),
        grid=(t // PROJ_TM, PROJ_STEPS),
        in_specs=[pl.BlockSpec((PROJ_TM, d), lambda i, j: (i, 0)),
                  pl.BlockSpec((1, d), lambda i, j: (0, 0)),
                  pl.BlockSpec((None, 9, d), lambda i, j: (i // tpb, 0, 0)),
                  pl.BlockSpec((None, d, sw), lambda i, j: (layer, 0, WP_SIG // sw + j)),
                  pl.BlockSpec((None, d, nw), lambda i, j: (layer, 0, WP_NORM // nw + j)),
                  pl.BlockSpec((None, d, pw), lambda i, j: (layer, 0, WP_PLAIN // pw + j)),
                  pl.BlockSpec((None, d, IF_COLS), lambda i, j: (layer, 0, 0)),
                  pl.BlockSpec((1, IF_COLS), lambda i, j: (0, 0)),
                  pl.BlockSpec((None, 1, HEAD_DIM_A), lambda i, j: (j // q_steps, 0, 0))],
        out_specs=(pl.BlockSpec((PROJ_TM, sw), lambda i, j: (i, j)),
                   pl.BlockSpec((PROJ_TM, nw), lambda i, j: (i, j)),
                   pl.BlockSpec((PROJ_TM, pw), lambda i, j: (i, j)),
                   pl.BlockSpec((PROJ_TM, IF_COLS), lambda i, j: (i, 0))),
        scratch_shapes=[pltpu.VMEM((PROJ_TM, d), BF16)],
        compiler_params=_params(("parallel", "arbitrary")),
        name="proj",
    )(x2, g, mod, w_p, w_p, w_p, w_if, b_if, qk_gain)


def _bias_kernel(g_ref, o_ref):
    r = lax.broadcasted_iota(jnp.int32, (ATT_TQ, ATT_W), 0)
    c = lax.broadcasted_iota(jnp.int32, (ATT_TQ, ATT_W), 1)
    for v in range(ATT_NVAR):
        x = jnp.broadcast_to(g_ref[v:v + 1, :], (ATT_TQ, ATT_ROLL))
        y = pltpu.roll(x, 0, 1, stride=1, stride_axis=0)[:, :ATT_W]
        dq = jnp.right_shift(r + v * ATT_TQ, CHUNK_LOG2) - jnp.right_shift(c, CHUNK_LOG2)
        valid = jnp.logical_and(dq >= 0, dq <= LEFT_CHUNKS)
        o_ref[v] = jnp.where(valid, y, NEG)


def _attn_bias(table):
    h = table.shape[0]
    p = np.arange(ATT_ROLL)
    u = np.where(p < ATT_W, p, p - ATT_ROLL)
    idx = np.stack([np.clip(v * ATT_TQ - u, -(CHUNK - 1), MAX_REL) + (CHUNK - 1) for v in range(ATT_NVAR)])
    g = jnp.take(table, jnp.asarray(idx.reshape(-1), jnp.int32), axis=1).reshape(h, ATT_NVAR, ATT_ROLL)
    return pl.pallas_call(
        _bias_kernel,
        out_shape=jax.ShapeDtypeStruct((h, ATT_NVAR, ATT_TQ, ATT_W), F32),
        grid=(h,),
        in_specs=[pl.BlockSpec((None, ATT_NVAR, ATT_ROLL), lambda i: (i, 0, 0))],
        out_specs=pl.BlockSpec((None, ATT_NVAR, ATT_TQ, ATT_W), lambda i: (i, 0, 0, 0)),
        compiler_params=_params(("parallel",)),
        name="attn_bias",
    )(g)


def _attn_kernel(q_ref, k_ref, v_ref, bias_ref, o_ref):
    qi = pl.program_id(2)
    var = jnp.minimum(qi, ATT_NVAR - 1)
    ks = pl.multiple_of((qi - var) * ATT_TQ, ATT_TQ)
    for hd in range(ATT_HG):
        cols = slice(hd * HEAD_DIM_A, (hd + 1) * HEAD_DIM_A)
        q = q_ref[:, cols]
        k = k_ref[pl.ds(ks, ATT_W), cols]
        v = v_ref[pl.ds(ks, ATT_W), cols]
        s = lax.dot_general(q, k, (((1,), (1,)), ((), ())), preferred_element_type=F32)
        s = s + bias_ref[hd, var]
        p = jnp.exp(s - jnp.max(s, axis=-1, keepdims=True))
        l = jnp.sum(p, axis=-1, keepdims=True)
        o = jnp.dot(p.astype(BF16), v, preferred_element_type=F32)
        o_ref[:, cols] = (o / l).astype(BF16)


def _attention(norm3, plain3, bias):
    b, s, _ = norm3.shape
    gw = ATT_HG * HEAD_DIM_A
    nvar = bias.shape[1]
    return pl.pallas_call(
        _attn_kernel,
        out_shape=jax.ShapeDtypeStruct((b, s, D_ATTN), BF16),
        grid=(b, N_HEADS_A // ATT_HG, s // ATT_TQ),
        in_specs=[pl.BlockSpec((None, ATT_TQ, gw), lambda bi, g, qi: (bi, qi, NORM_QA // gw + g)),
                  pl.BlockSpec((None, s, gw), lambda bi, g, qi: (bi, 0, NORM_KA // gw + g)),
                  pl.BlockSpec((None, s, gw), lambda bi, g, qi: (bi, 0, PLAIN_VA // gw + g)),
                  pl.BlockSpec((ATT_HG, nvar, ATT_TQ, ATT_W), lambda bi, g, qi: (g, 0, 0, 0))],
        out_specs=pl.BlockSpec((None, ATT_TQ, gw), lambda bi, g, qi: (bi, qi, g)),
        compiler_params=_params(("parallel", "parallel", "arbitrary")),
        name="attn",
    )(norm3, norm3, plain3, bias)


def _mlstm_kernel(uq_ref, uk_ref, v_ref, o_ref, zif_ref, cwq_ref, cwk_ref, cbq_ref, cbk_ref, g_ref, shift_ref, out_ref,
                  c_ref, n_ref, m_ref, tq_ref, tk_ref):
    L = ML_L
    ci = pl.program_id(1)

    @pl.when(ci == 0)
    def _():
        c_ref[...] = jnp.zeros_like(c_ref)
        n_ref[...] = jnp.zeros_like(n_ref)
        m_ref[...] = jnp.zeros_like(m_ref)
        tq_ref[...] = jnp.zeros_like(tq_ref)
        tk_ref[...] = jnp.zeros_like(tk_ref)

    row8 = lax.broadcasted_iota(jnp.int32, (8, D_MLSTM), 0)

    def conv_silu(u_ref, tail_ref, cw_ref, cb_ref):
        u = u_ref[...]
        uf = u.astype(F32)
        shifted = jnp.dot(shift_ref[...], u, preferred_element_type=F32)
        acc = cb_ref[...] + uf * cw_ref[CONV_W - 1:CONV_W, :]
        tail = tail_ref[...]
        head = jnp.zeros((8, D_MLSTM), F32)
        for dlt in range(1, CONV_W):
            w = cw_ref[CONV_W - 1 - dlt:CONV_W - dlt, :]
            acc = acc + shifted[(dlt - 1) * L:dlt * L, :] * w
            head = head + jnp.where(row8 < dlt, pltpu.roll(tail, dlt, axis=0), 0.0) * w
        acc = jnp.concatenate([acc[0:8, :] + head, acc[8:, :]], axis=0)
        tail_ref[...] = uf[L - 8:L, :]
        return acc * jax.nn.sigmoid(acc)

    q_all = conv_silu(uq_ref, tq_ref, cwq_ref, cbq_ref)
    k_all = conv_silu(uk_ref, tk_ref, cwk_ref, cbk_ref) * (HEAD_DIM_M ** -0.5)

    zi = zif_ref[...]
    logf = jnp.minimum(zi, 0.0) - jnp.log1p(jnp.exp(-jnp.abs(zi)))
    row = lax.broadcasted_iota(jnp.int32, (L, L), 0)
    col = lax.broadcasted_iota(jnp.int32, (L, L), 1)
    causal = row >= col
    bcs = jnp.dot(causal.astype(F32), logf, preferred_element_type=F32,
                  precision=lax.Precision.HIGHEST)
    zi_t = zi.T
    bcs_t = bcs.T

    for hd in range(N_HEADS_M):
        cols = slice(hd * HEAD_DIM_M, (hd + 1) * HEAD_DIM_M)
        q = q_all[:, cols]
        k = k_all[:, cols]
        qb = q.astype(BF16)
        kb = k.astype(BF16)
        v = v_ref[:, cols]
        i_col = zi[:, hd:hd + 1]
        b_col = bcs[:, N_HEADS_M + hd:N_HEADS_M + hd + 1]
        i_row = zi_t[hd:hd + 1, :]
        b_row = bcs_t[N_HEADS_M + hd:N_HEADS_M + hd + 1, :]
        m_prev = m_ref[hd, 0:1, 0:1]
        c_prev = c_ref[hd]
        n_prev = n_ref[hd]

        d = jnp.where(causal, b_col + (i_row - b_row), NEG)
        m_inter = b_col + m_prev
        m_t = jnp.maximum(m_inter, jnp.max(d, axis=-1, keepdims=True))
        qk = lax.dot_general(qb, kb, (((1,), (1,)), ((), ())), preferred_element_type=F32)
        sm = jnp.exp(d - m_t) * qk
        w_inter = jnp.exp(m_inter - m_t)
        num = (jnp.dot(sm.astype(BF16), v, preferred_element_type=F32)
               + w_inter * jnp.dot(qb, c_prev.astype(BF16), preferred_element_type=F32))
        den = (jnp.sum(sm, axis=-1, keepdims=True)
               + w_inter * jnp.sum(q * n_prev, axis=-1, keepdims=True))
        hh = num / jnp.maximum(jnp.abs(den), jnp.exp(-m_t))

        b_last = b_col[L - 1:L, :]
        w_end = b_last - b_col + i_col
        g_end = jnp.max(w_end, axis=0, keepdims=True)
        m_new = jnp.maximum(b_last + m_prev, g_end)
        decay = jnp.exp(b_last + m_prev - m_new)
        inj = jnp.exp(g_end - m_new)
        kw = k * jnp.exp(w_end - g_end)
        a_c = lax.dot_general(kw.astype(BF16), v, (((0,), (0,)), ((), ())), preferred_element_type=F32)
        c_ref[hd] = decay * c_prev + inj * a_c
        n_ref[hd] = decay * n_prev + inj * jnp.sum(kw, axis=0, keepdims=True)
        m_ref[hd] = jnp.broadcast_to(m_new, m_ref.shape[1:])

        hn = hh * lax.rsqrt(jnp.mean(hh * hh, axis=-1, keepdims=True) + EPS) * g_ref[:, cols]
        out_ref[:, cols] = (o_ref[:, cols].astype(F32) * hn).astype(BF16)


def _row_shift_matrices():
    t = np.arange(ML_L)
    blocks = [(t[None, :] == t[:, None] - d) for d in range(1, CONV_W)]
    return jnp.asarray(np.concatenate(blocks, axis=0), BF16)


def _mlstm(plain3, sig3, zif3, conv_w, conv_b, m_g):
    b, s, _ = plain3.shape
    L = ML_L
    dm = D_MLSTM
    return pl.pallas_call(
        _mlstm_kernel,
        out_shape=jax.ShapeDtypeStruct((b, s, dm), BF16),
        grid=(b, s // L),
        in_specs=[pl.BlockSpec((None, L, dm), lambda bi, ci: (bi, ci, PLAIN_QM // dm)),
                  pl.BlockSpec((None, L, dm), lambda bi, ci: (bi, ci, PLAIN_KM // dm)),
                  pl.BlockSpec((None, L, dm), lambda bi, ci: (bi, ci, PLAIN_VM // dm)),
                  pl.BlockSpec((None, L, dm), lambda bi, ci: (bi, ci, SIG_OM // dm)),
                  pl.BlockSpec((None, L, IF_COLS), lambda bi, ci: (bi, ci, 0)),
                  pl.BlockSpec((CONV_W, dm), lambda bi, ci: (0, 0)),
                  pl.BlockSpec((CONV_W, dm), lambda bi, ci: (0, 1)),
                  pl.BlockSpec((1, dm), lambda bi, ci: (0, 0)),
                  pl.BlockSpec((1, dm), lambda bi, ci: (0, 1)),
                  pl.BlockSpec((1, dm), lambda bi, ci: (0, 0)),
                  pl.BlockSpec(((CONV_W - 1) * L, L), lambda bi, ci: (0, 0))],
        out_specs=pl.BlockSpec((None, L, dm), lambda bi, ci: (bi, ci, 0)),
        scratch_shapes=[pltpu.VMEM((N_HEADS_M, HEAD_DIM_M, HEAD_DIM_M), F32),
                        pltpu.VMEM((N_HEADS_M, 1, HEAD_DIM_M), F32),
                        pltpu.VMEM((N_HEADS_M, 8, 128), F32),
                        pltpu.VMEM((8, dm), F32),
                        pltpu.VMEM((8, dm), F32)],
        compiler_params=_params(("parallel", "arbitrary")),
        name="mlstm",
    )(plain3, plain3, plain3, sig3, zif3, conv_w, conv_w, conv_b, conv_b, m_g, _row_shift_matrices())


def _merge_kernel(x_ref, mod_ref, a_ref, hm_ref, ga_ref, gm_ref, wa_ref, wm_ref, wo_ref, o_ref, *, row0):
    a = a_ref[...]
    hm = hm_ref[...]
    acc = None
    for j in range(D_MODEL // MERGE_TN):
        cols = slice(j * MERGE_TN, (j + 1) * MERGE_TN)
        ua = jnp.dot(a, wa_ref[:, cols], preferred_element_type=F32)
        um = jnp.dot(hm, wm_ref[:, cols], preferred_element_type=F32)
        merged = ga_ref[:, cols].astype(F32) * ua + gm_ref[:, cols].astype(F32) * um
        part = jnp.dot(merged.astype(BF16), wo_ref[cols, :], preferred_element_type=F32)
        acc = part if acc is None else acc + part
    o_ref[...] = x_ref[...] + mod_ref[row0 + 2:row0 + 3, :] * acc


def _merge(x2, mod, attn2, hm2, sig2, w_up_a, w_up_m, w_out, layer, row0, seq):
    t, d = x2.shape
    tpb = seq // MERGE_TM
    resident = pl.Buffered(1)
    return pl.pallas_call(
        functools.partial(_merge_kernel, row0=row0),
        out_shape=jax.ShapeDtypeStruct((t, d), F32),
        grid=(t // MERGE_TM,),
        in_specs=[pl.BlockSpec((MERGE_TM, d), lambda i: (i, 0)),
                  pl.BlockSpec((None, 9, d), lambda i: (i // tpb, 0, 0)),
                  pl.BlockSpec((MERGE_TM, D_ATTN), lambda i: (i, 0)),
                  pl.BlockSpec((MERGE_TM, D_MLSTM), lambda i: (i, 0)),
                  pl.BlockSpec((MERGE_TM, d), lambda i: (i, SIG_GA // d)),
                  pl.BlockSpec((MERGE_TM, d), lambda i: (i, SIG_GM // d)),
                  pl.BlockSpec((None, D_ATTN, d), lambda i: (layer, 0, 0), pipeline_mode=resident),
                  pl.BlockSpec((None, D_MLSTM, d), lambda i: (layer, 0, 0), pipeline_mode=resident),
                  pl.BlockSpec((None, d, d), lambda i: (layer, 0, 0), pipeline_mode=resident)],
        out_specs=pl.BlockSpec((MERGE_TM, d), lambda i: (i, 0)),
        compiler_params=_params(("parallel",), VMEM_LIMIT_MERGE),
        name="merge",
    )(x2, mod, attn2, hm2, sig2, sig2, w_up_a, w_up_m, w_out)


def kernel(x, c, norm_g, w_ada, b_ada, ffn1_w1, ffn1_w3, ffn1_w2, w_in, b_if, conv_w, conv_b, q_norm_g, k_norm_g,
           rel_table, m_norm_g, w_up_a, w_up_m, w_out, ffn2_w1, ffn2_w3, ffn2_w2):
    bsz, seq, d = x.shape
    depth = w_ada.shape[0]
    t = bsz * seq
    assert d == D_MODEL and seq % FFN_TM == 0 and seq % ML_L == 0 and seq >= ATT_W and bsz <= 8

    c8 = jnp.zeros((8, d), F32).at[:bsz].set(c)
    mod_all = _ada(c8, w_ada, b_ada)[:, :bsz].reshape(depth, bsz, 9, d)

    f1 = (_cast_pad(ffn1_w1, d, D_FF_PAD, d, CAST_B), _cast_pad(ffn1_w3, d, D_FF_PAD, d, CAST_B),
          _cast_pad(ffn1_w2, D_FF_PAD, d, CAST_B, d))
    f2 = (_cast_pad(ffn2_w1, d, D_FF_PAD, d, CAST_B), _cast_pad(ffn2_w3, d, D_FF_PAD, d, CAST_B),
          _cast_pad(ffn2_w2, D_FF_PAD, d, CAST_B, d))
    w_p, w_if = _cast_win(w_in)
    wa = _cast_pad(w_up_a, D_ATTN, d, D_ATTN, d)
    wm = _cast_pad(w_up_m, D_MLSTM, d, D_MLSTM, d)
    wo = _cast_pad(w_out, d, d, d // 2, d)

    x2 = x.reshape(t, d)
    for l in range(depth):
        mod = mod_all[l]
        x2 = _ffn(x2, norm_g[l, 0:1], mod, *f1, l, 0, seq)

        bif = jnp.pad(b_if[l], (0, IF_COLS - N_IF)).reshape(1, IF_COLS)
        qk_gain = jnp.stack([q_norm_g[l] * HEAD_DIM_A ** -0.5, k_norm_g[l]]).reshape(2, 1, HEAD_DIM_A)
        sig2, norm2, plain2, zif2 = _proj(x2, norm_g[l, 1:2], mod, w_p, w_if, bif, qk_gain, l, 3, seq)
        plain3 = plain2.reshape(bsz, seq, PLAIN_COLS)
        attn = _attention(norm2.reshape(bsz, seq, NORM_COLS), plain3, _attn_bias(rel_table[l]))
        hm = _mlstm(plain3, sig2.reshape(bsz, seq, SIG_COLS), zif2.reshape(bsz, seq, IF_COLS), conv_w[l],
                    conv_b[l].reshape(1, -1), m_norm_g[l].reshape(1, -1))
        x2 = _merge(x2, mod, attn.reshape(t, D_ATTN), hm.reshape(t, D_MLSTM), sig2, wa, wm, wo, l, 3, seq)

        x2 = _ffn(x2, norm_g[l, 2:3], mod, *f2, l, 6, seq)
    return x2.reshape(bsz, seq, d)
```

```python
import functools

import numpy as np
import jax
import jax.numpy as jnp
from jax import lax
from jax.experimental import pallas as pl
from jax.experimental.pallas import tpu as pltpu

F32 = jnp.float32
BF16 = jnp.bfloat16

D_MODEL = 2048
CHUNK = 64
N_HEADS_A = 8
HEAD_DIM_A = 128
D_ATTN = N_HEADS_A * HEAD_DIM_A
LEFT_CHUNKS = 8
LEFT = LEFT_CHUNKS * CHUNK
MAX_REL = 4 * CHUNK
N_HEADS_M = 4
HEAD_DIM_M = 256
D_MLSTM = N_HEADS_M * HEAD_DIM_M
CONV_W = 4
D_FF = ((8 * D_MODEL // 3 + 127) // 128) * 128
EPS = 1e-6
NEG = -1e30

SIG_GA, SIG_GM, SIG_OM = 0, D_MODEL, 2 * D_MODEL
SIG_COLS = D_MLSTM + 2 * D_MODEL
NORM_QA, NORM_KA = 0, D_ATTN
NORM_COLS = 2 * D_ATTN
PLAIN_VA, PLAIN_QM, PLAIN_KM, PLAIN_VM = 0, D_ATTN, D_ATTN + D_MLSTM, D_ATTN + 2 * D_MLSTM
PLAIN_COLS = D_ATTN + 3 * D_MLSTM
WP_SIG, WP_NORM, WP_PLAIN = 0, SIG_COLS, SIG_COLS + NORM_COLS
Z_COLS = SIG_COLS + NORM_COLS + PLAIN_COLS
IF_COLS = 128
N_IF = 2 * N_HEADS_M
WIN_OM = 3 * D_ATTN + 3 * D_MLSTM
IF0 = WIN_OM + D_MLSTM

FFN_TM, FFN_TF = 1024, 512
D_FF_PAD = ((D_FF + FFN_TF - 1) // FFN_TF) * FFN_TF
PROJ_TM = 512
PROJ_STEPS = 4
PROJ_SIG_W, PROJ_NORM_W, PROJ_PLAIN_W = SIG_COLS // PROJ_STEPS, NORM_COLS // PROJ_STEPS, PLAIN_COLS // PROJ_STEPS
ATT_TQ = 256
ATT_W = LEFT + ATT_TQ
ATT_NVAR = LEFT // ATT_TQ + 1
ATT_ROLL = 1024
ATT_HG = 4
CHUNK_LOG2 = 6
assert ATT_W + ATT_TQ - 1 <= ATT_ROLL and 1 << CHUNK_LOG2 == CHUNK
ML_L = 256
MERGE_TM, MERGE_TN = 512, 512
ADA_TN = 1024
WIN_TN = 1024
WIN_J_GATE0, WIN_J_GATE1 = SIG_GA // WIN_TN, SIG_OM // WIN_TN
CAST_B = 512
MIB = 1024 * 1024
VMEM_LIMIT = 48 * MIB
VMEM_LIMIT_MERGE = 58 * MIB
VMEM_LIMIT_FFN = 60 * MIB
MOD_ROWS = 16
MOD_UNROLL = 8


def _params(sem, vmem_limit=VMEM_LIMIT):
    return pltpu.CompilerParams(dimension_semantics=sem, vmem_limit_bytes=vmem_limit)


def _modulate_into(h_ref, x_ref, g_ref, mod_ref, row0, p_ref):
    d = x_ref.shape[1]
    p_ref[0] = jnp.broadcast_to(g_ref[...], (MOD_ROWS, d))
    p_ref[1] = jnp.broadcast_to(1.0 + mod_ref[row0 + 1:row0 + 2, :], (MOD_ROWS, d))
    p_ref[2] = jnp.broadcast_to(mod_ref[row0:row0 + 1, :], (MOD_ROWS, d))

    def group(r, carry):
        rows = pl.ds(pl.multiple_of(r * MOD_ROWS, MOD_ROWS), MOD_ROWS)
        x = x_ref[rows, :]
        y = x * lax.rsqrt(jnp.mean(x * x, axis=-1, keepdims=True) + EPS)
        h = (y * p_ref[0]) * p_ref[1] + p_ref[2]
        h_ref[rows, :] = h.astype(BF16)
        return carry

    lax.fori_loop(0, x_ref.shape[0] // MOD_ROWS, group, 0, unroll=MOD_UNROLL)


def _ada_kernel(c_ref, w_ref, b_ref, o_ref):
    c = c_ref[...]
    ca = (c * jax.nn.sigmoid(c)).astype(BF16)
    o_ref[...] = jnp.dot(ca, w_ref[...].astype(BF16), preferred_element_type=F32) + b_ref[...]


def _ada(c8, w_ada, b_ada):
    depth, d, n = w_ada.shape
    return pl.pallas_call(
        _ada_kernel,
        out_shape=jax.ShapeDtypeStruct((depth, 8, n), F32),
        grid=(depth, n // ADA_TN),
        in_specs=[pl.BlockSpec((8, d), lambda l, j: (0, 0)),
                  pl.BlockSpec((None, d, ADA_TN), lambda l, j: (l, 0, j)),
                  pl.BlockSpec((None, 1, ADA_TN), lambda l, j: (l, 0, j))],
        out_specs=pl.BlockSpec((None, 8, ADA_TN), lambda l, j: (l, 0, j)),
        compiler_params=_params(("parallel", "parallel")),
        name="adaln",
    )(c8, w_ada, b_ada.reshape(depth, 1, n))


def _cast_kernel(w_ref, o_ref, *, rows, cols):
    br, bc = o_ref.shape
    x = w_ref[...]
    if rows % br or cols % bc:
        r = lax.broadcasted_iota(jnp.int32, (br, bc), 0) + pl.program_id(1) * br
        c = lax.broadcasted_iota(jnp.int32, (br, bc), 1) + pl.program_id(2) * bc
        x = jnp.where(jnp.logical_and(r < rows, c < cols), x, 0.0)
    o_ref[...] = x.astype(BF16)


def _cast_pad(w, rows_out, cols_out, br, bc):
    depth, rows, cols = w.shape
    return pl.pallas_call(
        functools.partial(_cast_kernel, rows=rows, cols=cols),
        out_shape=jax.ShapeDtypeStruct((depth, rows_out, cols_out), BF16),
        grid=(depth, rows_out // br, cols_out // bc),
        in_specs=[pl.BlockSpec((None, br, bc), lambda l, i, j: (l, i, j))],
        out_specs=pl.BlockSpec((None, br, bc), lambda l, i, j: (l, i, j)),
        compiler_params=_params(("parallel", "parallel", "parallel")),
        name="cast",
    )(w)


def _cast_win_kernel(a_ref, b_ref, wp_ref, wif_ref):
    j = pl.program_id(1)
    gates = jnp.logical_and(j >= WIN_J_GATE0, j < WIN_J_GATE1)

    @pl.when(jnp.logical_not(gates))
    def _():
        wp_ref[...] = a_ref[...].astype(BF16)

    @pl.when(gates)
    def _():
        full = jnp.concatenate([a_ref[...], b_ref[...]], axis=1)
        wp_ref[...] = pltpu.roll(full, full.shape[1] - N_IF, axis=1)[:, :WIN_TN].astype(BF16)

    @pl.when(j == WIN_J_GATE0)
    def _():
        lane = lax.broadcasted_iota(jnp.int32, wif_ref.shape, 1)
        wif_ref[...] = jnp.where(lane < N_IF, a_ref[:, :IF_COLS], 0.0).astype(BF16)


def _win_block(j):
    j_om = WIN_OM // WIN_TN
    return jnp.where(j < WIN_J_GATE1, j + j_om + 1, jnp.where(j == WIN_J_GATE1, j_om, j - (WIN_J_GATE1 + 1)))


def _cast_win(w_in):
    depth, d, _ = w_in.shape
    per = WIN_TN // IF_COLS

    def next_block(l, j):
        jg = jnp.clip(j, WIN_J_GATE0, WIN_J_GATE1 - 1)
        return (l, 0, (_win_block(jg) + 1) * per)

    return pl.pallas_call(
        _cast_win_kernel,
        out_shape=(jax.ShapeDtypeStruct((depth, d, Z_COLS), BF16), jax.ShapeDtypeStruct((depth, d, IF_COLS), BF16)),
        grid=(depth, Z_COLS // WIN_TN),
        in_specs=[pl.BlockSpec((None, d, WIN_TN), lambda l, j: (l, 0, _win_block(j))),
                  pl.BlockSpec((None, d, IF_COLS), next_block)],
        out_specs=(pl.BlockSpec((None, d, WIN_TN), lambda l, j: (l, 0, j)),
                   pl.BlockSpec((None, d, IF_COLS), lambda l, j: (l, 0, 0))),
        compiler_params=_params(("parallel", "arbitrary")),
        name="cast_win",
    )(w_in, w_in)


def _ffn_kernel(x_ref, g_ref, mod_ref, w1_ref, w3_ref, w2_ref, o_ref, h_ref, p_ref, *, row0):
    j = pl.program_id(1)

    @pl.when(j == 0)
    def _():
        _modulate_into(h_ref, x_ref, g_ref, mod_ref, row0, p_ref)

    h = h_ref[...]
    a = jnp.dot(h, w1_ref[...], preferred_element_type=F32)
    b = jnp.dot(h, w3_ref[...], preferred_element_type=F32)
    u = (a * jax.nn.sigmoid(a)) * b
    prev = jnp.where(j == 0, 0.0, o_ref[...])
    o_ref[...] = prev + jnp.dot(u.astype(BF16), w2_ref[...], preferred_element_type=F32)

    @pl.when(j == pl.num_programs(1) - 1)
    def _():
        o_ref[...] = x_ref[...] + (0.5 * mod_ref[row0 + 2:row0 + 3, :]) * o_ref[...]


def _ffn(x2, g, mod, w1, w3, w2, layer, row0, seq):
    t, d = x2.shape
    tpb = seq // FFN_TM
    nf = w1.shape[2] // FFN_TF
    return pl.pallas_call(
        functools.partial(_ffn_kernel, row0=row0),
        out_shape=jax.ShapeDtypeStruct((t, d), F32),
        grid=(t // FFN_TM, nf),
        in_specs=[pl.BlockSpec((FFN_TM, d), lambda i, j: (i, 0)),
                  pl.BlockSpec((1, d), lambda i, j: (0, 0)),
                  pl.BlockSpec((None, 9, d), lambda i, j: (i // tpb, 0, 0)),
                  pl.BlockSpec((None, d, FFN_TF), lambda i, j: (layer, 0, j)),
                  pl.BlockSpec((None, d, FFN_TF), lambda i, j: (layer, 0, j)),
                  pl.BlockSpec((None, FFN_TF, d), lambda i, j: (layer, j, 0))],
        out_specs=pl.BlockSpec((FFN_TM, d), lambda i, j: (i, 0)),
        scratch_shapes=[pltpu.VMEM((FFN_TM, d), BF16), pltpu.VMEM((3, MOD_ROWS, d), F32)],
        compiler_params=_params(("parallel", "arbitrary"), VMEM_LIMIT_FFN),
        name="ffn",
    )(x2, g, mod, w1, w3, w2)


def _proj_kernel(x_ref, g_ref, mod_ref, ws_ref, wn_ref, wp_ref, wif_ref, bif_ref, gain_ref,
                 sig_ref, norm_ref, plain_ref, zif_ref, h_ref, p_ref, *, row0):
    @pl.when(pl.program_id(1) == 0)
    def _():
        _modulate_into(h_ref, x_ref, g_ref, mod_ref, row0, p_ref)
        zif_ref[...] = jnp.dot(h_ref[...], wif_ref[...], preferred_element_type=F32) + bif_ref[...]

    h = h_ref[...]
    sig_ref[...] = jax.nn.sigmoid(jnp.dot(h, ws_ref[...], preferred_element_type=F32)).astype(BF16)
    qk = jnp.dot(h, wn_ref[...], preferred_element_type=F32)
    for hd in range(PROJ_NORM_W // HEAD_DIM_A):
        cols = slice(hd * HEAD_DIM_A, (hd + 1) * HEAD_DIM_A)
        a = qk[:, cols]
        y = a * lax.rsqrt(jnp.mean(a * a, axis=-1, keepdims=True) + EPS)
        norm_ref[:, cols] = (y * gain_ref[...]).astype(BF16)
    plain_ref[...] = jnp.dot(h, wp_ref[...], preferred_element_type=F32).astype(BF16)


def _proj(x2, g, mod, w_p, w_if, b_if, qk_gain, layer, row0, seq):
    t, d = x2.shape
    tpb = seq // PROJ_TM
    sw, nw, pw = PROJ_SIG_W, PROJ_NORM_W, PROJ_PLAIN_W
    q_steps = D_ATTN // nw
    return pl.pallas_call(
        functools.partial(_proj_kernel, row0=row0),
        out_shape=(jax.ShapeDtypeStruct((t, SIG_COLS), BF16), jax.ShapeDtypeStruct((t, NORM_COLS), BF16),
                   jax.ShapeDtypeStruct((t, PLAIN_COLS), BF16), jax.ShapeDtypeStruct((t, IF_COLS), F32)),
        grid=(t // PROJ_TM, PROJ_STEPS),
        in_specs=[pl.BlockSpec((PROJ_TM, d), lambda i, j: (i, 0)),
                  pl.BlockSpec((1, d), lambda i, j: (0, 0)),
                  pl.BlockSpec((None, 9, d), lambda i, j: (i // tpb, 0, 0)),
                  pl.BlockSpec((None, d, sw), lambda i, j: (layer, 0, WP_SIG // sw + j)),
                  pl.BlockSpec((None, d, nw), lambda i, j: (layer, 0, WP_NORM // nw + j)),
                  pl.BlockSpec((None, d, pw), lambda i, j: (layer, 0, WP_PLAIN // pw + j)),
                  pl.BlockSpec((None, d, IF_COLS), lambda i, j: (layer, 0, 0)),
                  pl.BlockSpec((1, IF_COLS), lambda i, j: (0, 0)),
                  pl.BlockSpec((None, 1, HEAD_DIM_A), lambda i, j: (j // q_steps, 0, 0))],
        out_specs=(pl.BlockSpec((PROJ_TM, sw), lambda i, j: (i, j)),
                   pl.BlockSpec((PROJ_TM, nw), lambda i, j: (i, j)),
                   pl.BlockSpec((PROJ_TM, pw), lambda i, j: (i, j)),
                   pl.BlockSpec((PROJ_TM, IF_COLS), lambda i, j: (i, 0))),
        scratch_shapes=[pltpu.VMEM((PROJ_TM, d), BF16), pltpu.VMEM((3, MOD_ROWS, d), F32)],
        compiler_params=_params(("parallel", "arbitrary")),
        name="proj",
    )(x2, g, mod, w_p, w_p, w_p, w_if, b_if, qk_gain)


def _bias_kernel(g_ref, o_ref):
    r = lax.broadcasted_iota(jnp.int32, (ATT_TQ, ATT_W), 0)
    c = lax.broadcasted_iota(jnp.int32, (ATT_TQ, ATT_W), 1)
    for v in range(ATT_NVAR):
        x = jnp.broadcast_to(g_ref[v:v + 1, :], (ATT_TQ, ATT_ROLL))
        y = pltpu.roll(x, 0, 1, stride=1, stride_axis=0)[:, :ATT_W]
        dq = jnp.right_shift(r + v * ATT_TQ, CHUNK_LOG2) - jnp.right_shift(c, CHUNK_LOG2)
        valid = jnp.logical_and(dq >= 0, dq <= LEFT_CHUNKS)
        o_ref[v] = jnp.where(valid, y, NEG)


def _attn_bias(table):
    h = table.shape[0]
    p = np.arange(ATT_ROLL)
    u = np.where(p < ATT_W, p, p - ATT_ROLL)
    idx = np.stack([np.clip(v * ATT_TQ - u, -(CHUNK - 1), MAX_REL) + (CHUNK - 1) for v in range(ATT_NVAR)])
    g = jnp.take(table, jnp.asarray(idx.reshape(-1), jnp.int32), axis=1).reshape(h, ATT_NVAR, ATT_ROLL)
    return pl.pallas_call(
        _bias_kernel,
        out_shape=jax.ShapeDtypeStruct((h, ATT_NVAR, ATT_TQ, ATT_W), F32),
        grid=(h,),
        in_specs=[pl.BlockSpec((None, ATT_NVAR, ATT_ROLL), lambda i: (i, 0, 0))],
        out_specs=pl.BlockSpec((None, ATT_NVAR, ATT_TQ, ATT_W), lambda i: (i, 0, 0, 0)),
        compiler_params=_params(("parallel",)),
        name="attn_bias",
    )(g)


def _attn_kernel(q_ref, k_ref, v_ref, bias_ref, o_ref):
    qi = pl.program_id(2)
    var = jnp.minimum(qi, ATT_NVAR - 1)
    ks = pl.multiple_of((qi - var) * ATT_TQ, ATT_TQ)
    for hd in range(ATT_HG):
        cols = slice(hd * HEAD_DIM_A, (hd + 1) * HEAD_DIM_A)
        q = q_ref[:, cols]
        k = k_ref[pl.ds(ks, ATT_W), cols]
        v = v_ref[pl.ds(ks, ATT_W), cols]
        s = lax.dot_general(q, k, (((1,), (1,)), ((), ())), preferred_element_type=F32)
        s = s + bias_ref[hd, var]
        p = jnp.exp(s - jnp.max(s, axis=-1, keepdims=True))
        l = jnp.sum(p, axis=-1, keepdims=True)
        o = jnp.dot(p.astype(BF16), v, preferred_element_type=F32)
        o_ref[:, cols] = (o / l).astype(BF16)


def _attention(norm3, plain3, bias):
    b, s, _ = norm3.shape
    gw = ATT_HG * HEAD_DIM_A
    nvar = bias.shape[1]
    return pl.pallas_call(
        _attn_kernel,
        out_shape=jax.ShapeDtypeStruct((b, s, D_ATTN), BF16),
        grid=(b, N_HEADS_A // ATT_HG, s // ATT_TQ),
        in_specs=[pl.BlockSpec((None, ATT_TQ, gw), lambda bi, g, qi: (bi, qi, NORM_QA // gw + g)),
                  pl.BlockSpec((None, s, gw), lambda bi, g, qi: (bi, 0, NORM_KA // gw + g)),
                  pl.BlockSpec((None, s, gw), lambda bi, g, qi: (bi, 0, PLAIN_VA // gw + g)),
                  pl.BlockSpec((ATT_HG, nvar, ATT_TQ, ATT_W), lambda bi, g, qi: (g, 0, 0, 0))],
        out_specs=pl.BlockSpec((None, ATT_TQ, gw), lambda bi, g, qi: (bi, qi, g)),
        compiler_params=_params(("parallel", "parallel", "arbitrary")),
        name="attn",
    )(norm3, norm3, plain3, bias)


def _mlstm_kernel(uq_ref, uk_ref, v_ref, o_ref, zif_ref, cwq_ref, cwk_ref, cbq_ref, cbk_ref, g_ref, shift_ref, out_ref,
                  c_ref, n_ref, m_ref, tq_ref, tk_ref):
    L = ML_L
    ci = pl.program_id(1)

    @pl.when(ci == 0)
    def _():
        c_ref[...] = jnp.zeros_like(c_ref)
        n_ref[...] = jnp.zeros_like(n_ref)
        m_ref[...] = jnp.zeros_like(m_ref)
        tq_ref[...] = jnp.zeros_like(tq_ref)
        tk_ref[...] = jnp.zeros_like(tk_ref)

    row8 = lax.broadcasted_iota(jnp.int32, (8, HEAD_DIM_M), 0)

    def conv_silu(u_ref, tail_ref, cw_ref, cb_ref, cols):
        u = u_ref[:, cols]
        uf = u.astype(F32)
        shifted = jnp.dot(shift_ref[...], u, preferred_element_type=F32)
        acc = cb_ref[:, cols] + uf * cw_ref[CONV_W - 1:CONV_W, cols]
        tail = tail_ref[:, cols]
        head = jnp.zeros((8, HEAD_DIM_M), F32)
        for dlt in range(1, CONV_W):
            w = cw_ref[CONV_W - 1 - dlt:CONV_W - dlt, cols]
            acc = acc + shifted[(dlt - 1) * L:dlt * L, :] * w
            head = head + jnp.where(row8 < dlt, pltpu.roll(tail, dlt, axis=0), 0.0) * w
        acc = jnp.concatenate([acc[0:8, :] + head, acc[8:, :]], axis=0)
        tail_ref[:, cols] = uf[L - 8:L, :]
        return acc * jax.nn.sigmoid(acc)

    zi = zif_ref[...]
    logf = jnp.minimum(zi, 0.0) - jnp.log1p(jnp.exp(-jnp.abs(zi)))
    row = lax.broadcasted_iota(jnp.int32, (L, L), 0)
    col = lax.broadcasted_iota(jnp.int32, (L, L), 1)
    causal = row >= col
    bcs = jnp.dot(causal.astype(F32), logf, preferred_element_type=F32,
                  precision=lax.Precision.HIGHEST)
    zi_t = zi.T
    bcs_t = bcs.T

    for hd in range(N_HEADS_M):
        cols = slice(hd * HEAD_DIM_M, (hd + 1) * HEAD_DIM_M)
        q = conv_silu(uq_ref, tq_ref, cwq_ref, cbq_ref, cols)
        k = conv_silu(uk_ref, tk_ref, cwk_ref, cbk_ref, cols) * (HEAD_DIM_M ** -0.5)
        qb = q.astype(BF16)
        kb = k.astype(BF16)
        v = v_ref[:, cols]
        i_col = zi[:, hd:hd + 1]
        b_col = bcs[:, N_HEADS_M + hd:N_HEADS_M + hd + 1]
        i_row = zi_t[hd:hd + 1, :]
        b_row = bcs_t[N_HEADS_M + hd:N_HEADS_M + hd + 1, :]
        m_prev = m_ref[hd, 0:1, 0:1]
        c_prev = c_ref[hd]
        n_prev = n_ref[hd]

        d = jnp.where(causal, b_col + (i_row - b_row), NEG)
        m_inter = b_col + m_prev
        m_t = jnp.maximum(m_inter, jnp.max(d, axis=-1, keepdims=True))
        qk = lax.dot_general(qb, kb, (((1,), (1,)), ((), ())), preferred_element_type=F32)
        sm = jnp.exp(d - m_t) * qk
        w_inter = jnp.exp(m_inter - m_t)
        num = (jnp.dot(sm.astype(BF16), v, preferred_element_type=F32)
               + w_inter * jnp.dot(qb, c_prev.astype(BF16), preferred_element_type=F32))
        den = (jnp.sum(sm, axis=-1, keepdims=True)
               + w_inter * jnp.sum(q * n_prev, axis=-1, keepdims=True))
        hh = num / jnp.maximum(jnp.abs(den), jnp.exp(-m_t))

        b_last = b_col[L - 1:L, :]
        w_end = b_last - b_col + i_col
        g_end = jnp.max(w_end, axis=0, keepdims=True)
        m_new = jnp.maximum(b_last + m_prev, g_end)
        decay = jnp.exp(b_last + m_prev - m_new)
        inj = jnp.exp(g_end - m_new)
        kw = k * jnp.exp(w_end - g_end)
        a_c = lax.dot_general(kw.astype(BF16), v, (((0,), (0,)), ((), ())), preferred_element_type=F32)
        c_ref[hd] = decay * c_prev + inj * a_c
        n_ref[hd] = decay * n_prev + inj * jnp.sum(kw, axis=0, keepdims=True)
        m_ref[hd] = jnp.broadcast_to(m_new, m_ref.shape[1:])

        hn = hh * lax.rsqrt(jnp.mean(hh * hh, axis=-1, keepdims=True) + EPS) * g_ref[:, cols]
        out_ref[:, cols] = (o_ref[:, cols].astype(F32) * hn).astype(BF16)


def _row_shift_matrices():
    t = np.arange(ML_L)
    blocks = [(t[None, :] == t[:, None] - d) for d in range(1, CONV_W)]
    return jnp.asarray(np.concatenate(blocks, axis=0), BF16)


def _mlstm(plain3, sig3, zif3, conv_w, conv_b, m_g):
    b, s, _ = plain3.shape
    L = ML_L
    dm = D_MLSTM
    return pl.pallas_call(
        _mlstm_kernel,
        out_shape=jax.ShapeDtypeStruct((b, s, dm), BF16),
        grid=(b, s // L),
        in_specs=[pl.BlockSpec((None, L, dm), lambda bi, ci: (bi, ci, PLAIN_QM // dm)),
                  pl.BlockSpec((None, L, dm), lambda bi, ci: (bi, ci, PLAIN_KM // dm)),
                  pl.BlockSpec((None, L, dm), lambda bi, ci: (bi, ci, PLAIN_VM // dm)),
                  pl.BlockSpec((None, L, dm), lambda bi, ci: (bi, ci, SIG_OM // dm)),
                  pl.BlockSpec((None, L, IF_COLS), lambda bi, ci: (bi, ci, 0)),
                  pl.BlockSpec((CONV_W, dm), lambda bi, ci: (0, 0)),
                  pl.BlockSpec((CONV_W, dm), lambda bi, ci: (0, 1)),
                  pl.BlockSpec((1, dm), lambda bi, ci: (0, 0)),
                  pl.BlockSpec((1, dm), lambda bi, ci: (0, 1)),
                  pl.BlockSpec((1, dm), lambda bi, ci: (0, 0)),
                  pl.BlockSpec(((CONV_W - 1) * L, L), lambda bi, ci: (0, 0))],
        out_specs=pl.BlockSpec((None, L, dm), lambda bi, ci: (bi, ci, 0)),
        scratch_shapes=[pltpu.VMEM((N_HEADS_M, HEAD_DIM_M, HEAD_DIM_M), F32),
                        pltpu.VMEM((N_HEADS_M, 1, HEAD_DIM_M), F32),
                        pltpu.VMEM((N_HEADS_M, 8, 128), F32),
                        pltpu.VMEM((8, dm), F32),
                        pltpu.VMEM((8, dm), F32)],
        compiler_params=_params(("parallel", "arbitrary")),
        name="mlstm",
    )(plain3, plain3, plain3, sig3, zif3, conv_w, conv_w, conv_b, conv_b, m_g, _row_shift_matrices())


def _merge_kernel(x_ref, mod_ref, a_ref, hm_ref, ga_ref, gm_ref, wa_ref, wm_ref, wo_ref, o_ref, *, row0):
    a = a_ref[...]
    hm = hm_ref[...]
    acc = None
    for j in range(D_MODEL // MERGE_TN):
        cols = slice(j * MERGE_TN, (j + 1) * MERGE_TN)
        ua = jnp.dot(a, wa_ref[:, cols], preferred_element_type=F32)
        um = jnp.dot(hm, wm_ref[:, cols], preferred_element_type=F32)
        merged = ga_ref[:, cols].astype(F32) * ua + gm_ref[:, cols].astype(F32) * um
        part = jnp.dot(merged.astype(BF16), wo_ref[cols, :], preferred_element_type=F32)
        acc = part if acc is None else acc + part
    o_ref[...] = x_ref[...] + mod_ref[row0 + 2:row0 + 3, :] * acc


def _merge(x2, mod, attn2, hm2, sig2, w_up_a, w_up_m, w_out, layer, row0, seq):
    t, d = x2.shape
    tpb = seq // MERGE_TM
    resident = pl.Buffered(1)
    return pl.pallas_call(
        functools.partial(_merge_kernel, row0=row0),
        out_shape=jax.ShapeDtypeStruct((t, d), F32),
        grid=(t // MERGE_TM,),
        in_specs=[pl.BlockSpec((MERGE_TM, d), lambda i: (i, 0)),
                  pl.BlockSpec((None, 9, d), lambda i: (i // tpb, 0, 0)),
                  pl.BlockSpec((MERGE_TM, D_ATTN), lambda i: (i, 0)),
                  pl.BlockSpec((MERGE_TM, D_MLSTM), lambda i: (i, 0)),
                  pl.BlockSpec((MERGE_TM, d), lambda i: (i, SIG_GA // d)),
                  pl.BlockSpec((MERGE_TM, d), lambda i: (i, SIG_GM // d)),
                  pl.BlockSpec((None, D_ATTN, d), lambda i: (layer, 0, 0), pipeline_mode=resident),
                  pl.BlockSpec((None, D_MLSTM, d), lambda i: (layer, 0, 0), pipeline_mode=resident),
                  pl.BlockSpec((None, d, d), lambda i: (layer, 0, 0), pipeline_mode=resident)],
        out_specs=pl.BlockSpec((MERGE_TM, d), lambda i: (i, 0)),
        compiler_params=_params(("parallel",), VMEM_LIMIT_MERGE),
        name="merge",
    )(x2, mod, attn2, hm2, sig2, sig2, w_up_a, w_up_m, w_out)


def kernel(x, c, norm_g, w_ada, b_ada, ffn1_w1, ffn1_w3, ffn1_w2, w_in, b_if, conv_w, conv_b, q_norm_g, k_norm_g,
           rel_table, m_norm_g, w_up_a, w_up_m, w_out, ffn2_w1, ffn2_w3, ffn2_w2):
    bsz, seq, d = x.shape
    depth = w_ada.shape[0]
    t = bsz * seq
    assert d == D_MODEL and seq % FFN_TM == 0 and seq % ML_L == 0 and seq >= ATT_W and bsz <= 8

    c8 = jnp.zeros((8, d), F32).at[:bsz].set(c)
    mod_all = _ada(c8, w_ada, b_ada)[:, :bsz].reshape(depth, bsz, 9, d)

    f1 = (_cast_pad(ffn1_w1, d, D_FF_PAD, d, CAST_B), _cast_pad(ffn1_w3, d, D_FF_PAD, d, CAST_B),
          _cast_pad(ffn1_w2, D_FF_PAD, d, CAST_B, d))
    f2 = (_cast_pad(ffn2_w1, d, D_FF_PAD, d, CAST_B), _cast_pad(ffn2_w3, d, D_FF_PAD, d, CAST_B),
          _cast_pad(ffn2_w2, D_FF_PAD, d, CAST_B, d))
    w_p, w_if = _cast_win(w_in)
    wa = _cast_pad(w_up_a, D_ATTN, d, D_ATTN, d)
    wm = _cast_pad(w_up_m, D_MLSTM, d, D_MLSTM, d)
    wo = _cast_pad(w_out, d, d, d // 2, d)

    x2 = x.reshape(t, d)
    for l in range(depth):
        mod = mod_all[l]
        x2 = _ffn(x2, norm_g[l, 0:1], mod, *f1, l, 0, seq)

        bif = jnp.pad(b_if[l], (0, IF_COLS - N_IF)).reshape(1, IF_COLS)
        qk_gain = jnp.stack([q_norm_g[l] * HEAD_DIM_A ** -0.5, k_norm_g[l]]).reshape(2, 1, HEAD_DIM_A)
        sig2, norm2, plain2, zif2 = _proj(x2, norm_g[l, 1:2], mod, w_p, w_if, bif, qk_gain, l, 3, seq)
        plain3 = plain2.reshape(bsz, seq, PLAIN_COLS)
        attn = _attention(norm2.reshape(bsz, seq, NORM_COLS), plain3, _attn_bias(rel_table[l]))
        hm = _mlstm(plain3, sig2.reshape(bsz, seq, SIG_COLS), zif2.reshape(bsz, seq, IF_COLS), conv_w[l],
                    conv_b[l].reshape(1, -1), m_norm_g[l].reshape(1, -1))
        x2 = _merge(x2, mod, attn.reshape(t, D_ATTN), hm.reshape(t, D_MLSTM), sig2, wa, wm, wo, l, 3, seq)

        x2 = _ffn(x2, norm_g[l, 2:3], mod, *f2, l, 6, seq)
    return x2.reshape(bsz, seq, d)
```

```python
import functools

import numpy as np
import jax
import jax.numpy as jnp
from jax import lax
from jax.experimental import pallas as pl
from jax.experimental.pallas import tpu as pltpu

F32 = jnp.float32
BF16 = jnp.bfloat16

D_MODEL = 2048
CHUNK = 64
N_HEADS_A = 8
HEAD_DIM_A = 128
D_ATTN = N_HEADS_A * HEAD_DIM_A
LEFT_CHUNKS = 8
LEFT = LEFT_CHUNKS * CHUNK
MAX_REL = 4 * CHUNK
N_HEADS_M = 4
HEAD_DIM_M = 256
D_MLSTM = N_HEADS_M * HEAD_DIM_M
CONV_W = 4
D_FF = ((8 * D_MODEL // 3 + 127) // 128) * 128
EPS = 1e-6
NEG = -1e30

SIG_GA, SIG_GM, SIG_OM = 0, D_MODEL, 2 * D_MODEL
SIG_COLS = D_MLSTM + 2 * D_MODEL
NORM_QA, NORM_KA = 0, D_ATTN
NORM_COLS = 2 * D_ATTN
PLAIN_VA, PLAIN_QM, PLAIN_KM, PLAIN_VM = 0, D_ATTN, D_ATTN + D_MLSTM, D_ATTN + 2 * D_MLSTM
PLAIN_COLS = D_ATTN + 3 * D_MLSTM
WP_SIG, WP_NORM, WP_PLAIN = 0, SIG_COLS, SIG_COLS + NORM_COLS
Z_COLS = SIG_COLS + NORM_COLS + PLAIN_COLS
IF_COLS = 128
N_IF = 2 * N_HEADS_M
WIN_OM = 3 * D_ATTN + 3 * D_MLSTM
IF0 = WIN_OM + D_MLSTM

FFN_TM, FFN_TF = 1024, 512
D_FF_PAD = ((D_FF + FFN_TF - 1) // FFN_TF) * FFN_TF
PROJ_TM = 512
PROJ_STEPS = 4
PROJ_SIG_W, PROJ_NORM_W, PROJ_PLAIN_W = SIG_COLS // PROJ_STEPS, NORM_COLS // PROJ_STEPS, PLAIN_COLS // PROJ_STEPS
ATT_TQ = 256
ATT_W = LEFT + ATT_TQ
ATT_NVAR = LEFT // ATT_TQ + 1
ATT_ROLL = 1024
ATT_HG = 4
CHUNK_LOG2 = 6
assert ATT_W + ATT_TQ - 1 <= ATT_ROLL and 1 << CHUNK_LOG2 == CHUNK
ML_L = 256
MERGE_TM, MERGE_TN = 512, 512
ADA_TN = 1024
WIN_TN = 1024
WIN_J_GATE0, WIN_J_GATE1 = SIG_GA // WIN_TN, SIG_OM // WIN_TN
CAST_B = 512
MIB = 1024 * 1024
VMEM_LIMIT = 48 * MIB
VMEM_LIMIT_MERGE = 58 * MIB
VMEM_LIMIT_FFN = 60 * MIB
MOD_ROWS = 16
MOD_UNROLL = 8


def _params(sem, vmem_limit=VMEM_LIMIT):
    return pltpu.CompilerParams(dimension_semantics=sem, vmem_limit_bytes=vmem_limit)


def _modulate_into(h_ref, x_ref, g_ref, mod_ref, row0):
    x = x_ref[...]
    y = x * lax.rsqrt(jnp.mean(x * x, axis=-1, keepdims=True) + EPS)
    h = (y * g_ref[...]) * (1.0 + mod_ref[row0 + 1:row0 + 2, :]) + mod_ref[row0:row0 + 1, :]
    h_ref[...] = h.astype(BF16)


def _modulate_rows_into(h_ref, x_ref, g_ref, mod_ref, row0, p_ref):
    d = x_ref.shape[1]
    p_ref[0] = jnp.broadcast_to(g_ref[...], (MOD_ROWS, d))
    p_ref[1] = jnp.broadcast_to(1.0 + mod_ref[row0 + 1:row0 + 2, :], (MOD_ROWS, d))
    p_ref[2] = jnp.broadcast_to(mod_ref[row0:row0 + 1, :], (MOD_ROWS, d))

    def group(r, carry):
        rows = pl.ds(pl.multiple_of(r * MOD_ROWS, MOD_ROWS), MOD_ROWS)
        x = x_ref[rows, :]
        y = x * lax.rsqrt(jnp.mean(x * x, axis=-1, keepdims=True) + EPS)
        h = (y * p_ref[0]) * p_ref[1] + p_ref[2]
        h_ref[rows, :] = h.astype(BF16)
        return carry

    lax.fori_loop(0, x_ref.shape[0] // MOD_ROWS, group, 0, unroll=MOD_UNROLL)


def _ada_kernel(c_ref, w_ref, b_ref, o_ref):
    c = c_ref[...]
    ca = (c * jax.nn.sigmoid(c)).astype(BF16)
    o_ref[...] = jnp.dot(ca, w_ref[...].astype(BF16), preferred_element_type=F32) + b_ref[...]


def _ada(c8, w_ada, b_ada):
    depth, d, n = w_ada.shape
    return pl.pallas_call(
        _ada_kernel,
        out_shape=jax.ShapeDtypeStruct((depth, 8, n), F32),
        grid=(depth, n // ADA_TN),
        in_specs=[pl.BlockSpec((8, d), lambda l, j: (0, 0)),
                  pl.BlockSpec((None, d, ADA_TN), lambda l, j: (l, 0, j)),
                  pl.BlockSpec((None, 1, ADA_TN), lambda l, j: (l, 0, j))],
        out_specs=pl.BlockSpec((None, 8, ADA_TN), lambda l, j: (l, 0, j)),
        compiler_params=_params(("parallel", "parallel")),
        name="adaln",
    )(c8, w_ada, b_ada.reshape(depth, 1, n))


def _cast_kernel(w_ref, o_ref, *, rows, cols):
    br, bc = o_ref.shape
    x = w_ref[...]
    if rows % br or cols % bc:
        r = lax.broadcasted_iota(jnp.int32, (br, bc), 0) + pl.program_id(1) * br
        c = lax.broadcasted_iota(jnp.int32, (br, bc), 1) + pl.program_id(2) * bc
        x = jnp.where(jnp.logical_and(r < rows, c < cols), x, 0.0)
    o_ref[...] = x.astype(BF16)


def _cast_pad(w, rows_out, cols_out, br, bc):
    depth, rows, cols = w.shape
    return pl.pallas_call(
        functools.partial(_cast_kernel, rows=rows, cols=cols),
        out_shape=jax.ShapeDtypeStruct((depth, rows_out, cols_out), BF16),
        grid=(depth, rows_out // br, cols_out // bc),
        in_specs=[pl.BlockSpec((None, br, bc), lambda l, i, j: (l, i, j))],
        out_specs=pl.BlockSpec((None, br, bc), lambda l, i, j: (l, i, j)),
        compiler_params=_params(("parallel", "parallel", "parallel")),
        name="cast",
    )(w)


def _cast_win_kernel(a_ref, b_ref, wp_ref, wif_ref):
    j = pl.program_id(1)
    gates = jnp.logical_and(j >= WIN_J_GATE0, j < WIN_J_GATE1)

    @pl.when(jnp.logical_not(gates))
    def _():
        wp_ref[...] = a_ref[...].astype(BF16)

    @pl.when(gates)
    def _():
        full = jnp.concatenate([a_ref[...], b_ref[...]], axis=1)
        wp_ref[...] = pltpu.roll(full, full.shape[1] - N_IF, axis=1)[:, :WIN_TN].astype(BF16)

    @pl.when(j == WIN_J_GATE0)
    def _():
        lane = lax.broadcasted_iota(jnp.int32, wif_ref.shape, 1)
        wif_ref[...] = jnp.where(lane < N_IF, a_ref[:, :IF_COLS], 0.0).astype(BF16)


def _win_block(j):
    j_om = WIN_OM // WIN_TN
    return jnp.where(j < WIN_J_GATE1, j + j_om + 1, jnp.where(j == WIN_J_GATE1, j_om, j - (WIN_J_GATE1 + 1)))


def _cast_win(w_in):
    depth, d, _ = w_in.shape
    per = WIN_TN // IF_COLS

    def next_block(l, j):
        jg = jnp.clip(j, WIN_J_GATE0, WIN_J_GATE1 - 1)
        return (l, 0, (_win_block(jg) + 1) * per)

    return pl.pallas_call(
        _cast_win_kernel,
        out_shape=(jax.ShapeDtypeStruct((depth, d, Z_COLS), BF16), jax.ShapeDtypeStruct((depth, d, IF_COLS), BF16)),
        grid=(depth, Z_COLS // WIN_TN),
        in_specs=[pl.BlockSpec((None, d, WIN_TN), lambda l, j: (l, 0, _win_block(j))),
                  pl.BlockSpec((None, d, IF_COLS), next_block)],
        out_specs=(pl.BlockSpec((None, d, WIN_TN), lambda l, j: (l, 0, j)),
                   pl.BlockSpec((None, d, IF_COLS), lambda l, j: (l, 0, 0))),
        compiler_params=_params(("parallel", "arbitrary")),
        name="cast_win",
    )(w_in, w_in)


def _ffn_kernel(x_ref, g_ref, mod_ref, w1_ref, w3_ref, w2_ref, o_ref, h_ref, p_ref, *, row0):
    j = pl.program_id(1)

    @pl.when(j == 0)
    def _():
        _modulate_rows_into(h_ref, x_ref, g_ref, mod_ref, row0, p_ref)

    h = h_ref[...]
    a = jnp.dot(h, w1_ref[...], preferred_element_type=F32)
    b = jnp.dot(h, w3_ref[...], preferred_element_type=F32)
    u = (a * jax.nn.sigmoid(a)) * b
    prev = jnp.where(j == 0, 0.0, o_ref[...])
    o_ref[...] = prev + jnp.dot(u.astype(BF16), w2_ref[...], preferred_element_type=F32)

    @pl.when(j == pl.num_programs(1) - 1)
    def _():
        o_ref[...] = x_ref[...] + (0.5 * mod_ref[row0 + 2:row0 + 3, :]) * o_ref[...]


def _ffn(x2, g, mod, w1, w3, w2, layer, row0, seq):
    t, d = x2.shape
    tpb = seq // FFN_TM
    nf = w1.shape[2] // FFN_TF
    return pl.pallas_call(
        functools.partial(_ffn_kernel, row0=row0),
        out_shape=jax.ShapeDtypeStruct((t, d), F32),
        grid=(t // FFN_TM, nf),
        in_specs=[pl.BlockSpec((FFN_TM, d), lambda i, j: (i, 0)),
                  pl.BlockSpec((1, d), lambda i, j: (0, 0)),
                  pl.BlockSpec((None, 9, d), lambda i, j: (i // tpb, 0, 0)),
                  pl.BlockSpec((None, d, FFN_TF), lambda i, j: (layer, 0, j)),
                  pl.BlockSpec((None, d, FFN_TF), lambda i, j: (layer, 0, j)),
                  pl.BlockSpec((None, FFN_TF, d), lambda i, j: (layer, j, 0))],
        out_specs=pl.BlockSpec((FFN_TM, d), lambda i, j: (i, 0)),
        scratch_shapes=[pltpu.VMEM((FFN_TM, d), BF16), pltpu.VMEM((3, MOD_ROWS, d), F32)],
        compiler_params=_params(("parallel", "arbitrary"), VMEM_LIMIT_FFN),
        name="ffn",
    )(x2, g, mod, w1, w3, w2)


def _proj_kernel(x_ref, g_ref, mod_ref, ws_ref, wn_ref, wp_ref, wif_ref, bif_ref, gain_ref,
                 sig_ref, norm_ref, plain_ref, zif_ref, h_ref, *, row0):
    @pl.when(pl.program_id(1) == 0)
    def _():
        _modulate_into(h_ref, x_ref, g_ref, mod_ref, row0)
        zif_ref[...] = jnp.dot(h_ref[...], wif_ref[...], preferred_element_type=F32) + bif_ref[...]

    h = h_ref[...]
    sig_ref[...] = jax.nn.sigmoid(jnp.dot(h, ws_ref[...], preferred_element_type=F32)).astype(BF16)
    qk = jnp.dot(h, wn_ref[...], preferred_element_type=F32)
    for hd in range(PROJ_NORM_W // HEAD_DIM_A):
        cols = slice(hd * HEAD_DIM_A, (hd + 1) * HEAD_DIM_A)
        a = qk[:, cols]
        y = a * lax.rsqrt(jnp.mean(a * a, axis=-1, keepdims=True) + EPS)
        norm_ref[:, cols] = (y * gain_ref[...]).astype(BF16)
    plain_ref[...] = jnp.dot(h, wp_ref[...], preferred_element_type=F32).astype(BF16)


def _proj(x2, g, mod, w_p, w_if, b_if, qk_gain, layer, row0, seq):
    t, d = x2.shape
    tpb = seq // PROJ_TM
    sw, nw, pw = PROJ_SIG_W, PROJ_NORM_W, PROJ_PLAIN_W
    q_steps = D_ATTN // nw
    return pl.pallas_call(
        functools.partial(_proj_kernel, row0=row0),
        out_shape=(jax.ShapeDtypeStruct((t, SIG_COLS), BF16), jax.ShapeDtypeStruct((t, NORM_COLS), BF16),
                   jax.ShapeDtypeStruct((t, PLAIN_COLS), BF16), jax.ShapeDtypeStruct((t, IF_COLS), F32)),
        grid=(t // PROJ_TM, PROJ_STEPS),
        in_specs=[pl.BlockSpec((PROJ_TM, d), lambda i, j: (i, 0)),
                  pl.BlockSpec((1, d), lambda i, j: (0, 0)),
                  pl.BlockSpec((None, 9, d), lambda i, j: (i // tpb, 0, 0)),
                  pl.BlockSpec((None, d, sw), lambda i, j: (layer, 0, WP_SIG // sw + j)),
                  pl.BlockSpec((None, d, nw), lambda i, j: (layer, 0, WP_NORM // nw + j)),
                  pl.BlockSpec((None, d, pw), lambda i, j: (layer, 0, WP_PLAIN // pw + j)),
                  pl.BlockSpec((None, d, IF_COLS), lambda i, j: (layer, 0, 0)),
                  pl.BlockSpec((1, IF_COLS), lambda i, j: (0, 0)),
                  pl.BlockSpec((None, 1, HEAD_DIM_A), lambda i, j: (j // q_steps, 0, 0))],
        out_specs=(pl.BlockSpec((PROJ_TM, sw), lambda i, j: (i, j)),
                   pl.BlockSpec((PROJ_TM, nw), lambda i, j: (i, j)),
                   pl.BlockSpec((PROJ_TM, pw), lambda i, j: (i, j)),
                   pl.BlockSpec((PROJ_TM, IF_COLS), lambda i, j: (i, 0))),
        scratch_shapes=[pltpu.VMEM((PROJ_TM, d), BF16)],
        compiler_params=_params(("parallel", "arbitrary")),
        name="proj",
    )(x2, g, mod, w_p, w_p, w_p, w_if, b_if, qk_gain)


def _bias_kernel(g_ref, o_ref):
    r = lax.broadcasted_iota(jnp.int32, (ATT_TQ, ATT_W), 0)
    c = lax.broadcasted_iota(jnp.int32, (ATT_TQ, ATT_W), 1)
    for v in range(ATT_NVAR):
        x = jnp.broadcast_to(g_ref[v:v + 1, :], (ATT_TQ, ATT_ROLL))
        y = pltpu.roll(x, 0, 1, stride=1, stride_axis=0)[:, :ATT_W]
        dq = jnp.right_shift(r + v * ATT_TQ, CHUNK_LOG2) - jnp.right_shift(c, CHUNK_LOG2)
        valid = jnp.logical_and(dq >= 0, dq <= LEFT_CHUNKS)
        o_ref[v] = jnp.where(valid, y, NEG)


def _attn_bias(table):
    h = table.shape[0]
    p = np.arange(ATT_ROLL)
    u = np.where(p < ATT_W, p, p - ATT_ROLL)
    idx = np.stack([np.clip(v * ATT_TQ - u, -(CHUNK - 1), MAX_REL) + (CHUNK - 1) for v in range(ATT_NVAR)])
    g = jnp.take(table, jnp.asarray(idx.reshape(-1), jnp.int32), axis=1).reshape(h, ATT_NVAR, ATT_ROLL)
    return pl.pallas_call(
        _bias_kernel,
        out_shape=jax.ShapeDtypeStruct((h, ATT_NVAR, ATT_TQ, ATT_W), F32),
        grid=(h,),
        in_specs=[pl.BlockSpec((None, ATT_NVAR, ATT_ROLL), lambda i: (i, 0, 0))],
        out_specs=pl.BlockSpec((None, ATT_NVAR, ATT_TQ, ATT_W), lambda i: (i, 0, 0, 0)),
        compiler_params=_params(("parallel",)),
        name="attn_bias",
    )(g)


def _attn_kernel(q_ref, k_ref, v_ref, bias_ref, o_ref):
    qi = pl.program_id(2)
    var = jnp.minimum(qi, ATT_NVAR - 1)
    ks = pl.multiple_of((qi - var) * ATT_TQ, ATT_TQ)

    def scores(hd):
        cols = slice(hd * HEAD_DIM_A, (hd + 1) * HEAD_DIM_A)
        q = q_ref[:, cols]
        k = k_ref[pl.ds(ks, ATT_W), cols]
        s = lax.dot_general(q, k, (((1,), (1,)), ((), ())), preferred_element_type=F32)
        return s + bias_ref[hd, var]

    s_next = scores(0)
    for hd in range(ATT_HG):
        cols = slice(hd * HEAD_DIM_A, (hd + 1) * HEAD_DIM_A)
        s = s_next
        if hd + 1 < ATT_HG:
            s_next = scores(hd + 1)
        p = jnp.exp(s - jnp.max(s, axis=-1, keepdims=True))
        l = jnp.sum(p, axis=-1, keepdims=True)
        v = v_ref[pl.ds(ks, ATT_W), cols]
        o = jnp.dot(p.astype(BF16), v, preferred_element_type=F32)
        o_ref[:, cols] = (o / l).astype(BF16)


def _attention(norm3, plain3, bias):
    b, s, _ = norm3.shape
    gw = ATT_HG * HEAD_DIM_A
    nvar = bias.shape[1]
    return pl.pallas_call(
        _attn_kernel,
        out_shape=jax.ShapeDtypeStruct((b, s, D_ATTN), BF16),
        grid=(N_HEADS_A // ATT_HG, b, s // ATT_TQ),
        in_specs=[pl.BlockSpec((None, ATT_TQ, gw), lambda g, bi, qi: (bi, qi, NORM_QA // gw + g)),
                  pl.BlockSpec((None, s, gw), lambda g, bi, qi: (bi, 0, NORM_KA // gw + g)),
                  pl.BlockSpec((None, s, gw), lambda g, bi, qi: (bi, 0, PLAIN_VA // gw + g)),
                  pl.BlockSpec((ATT_HG, nvar, ATT_TQ, ATT_W), lambda g, bi, qi: (g, 0, 0, 0))],
        out_specs=pl.BlockSpec((None, ATT_TQ, gw), lambda g, bi, qi: (bi, qi, g)),
        compiler_params=_params(("parallel", "parallel", "arbitrary")),
        name="attn",
    )(norm3, norm3, plain3, bias)


def _mlstm_kernel(uq_ref, uk_ref, v_ref, o_ref, zif_ref, cwq_ref, cwk_ref, cbq_ref, cbk_ref, g_ref, shift_ref, out_ref,
                  c_ref, n_ref, m_ref, tq_ref, tk_ref):
    L = ML_L
    ci = pl.program_id(1)

    @pl.when(ci == 0)
    def _():
        c_ref[...] = jnp.zeros_like(c_ref)
        n_ref[...] = jnp.zeros_like(n_ref)
        m_ref[...] = jnp.zeros_like(m_ref)
        tq_ref[...] = jnp.zeros_like(tq_ref)
        tk_ref[...] = jnp.zeros_like(tk_ref)

    row8 = lax.broadcasted_iota(jnp.int32, (8, HEAD_DIM_M), 0)

    def conv_silu(u_ref, tail_ref, cw_ref, cb_ref, cols):
        u = u_ref[:, cols]
        uf = u.astype(F32)
        shifted = jnp.dot(shift_ref[...], u, preferred_element_type=F32)
        acc = cb_ref[:, cols] + uf * cw_ref[CONV_W - 1:CONV_W, cols]
        tail = tail_ref[:, cols]
        head = jnp.zeros((8, HEAD_DIM_M), F32)
        for dlt in range(1, CONV_W):
            w = cw_ref[CONV_W - 1 - dlt:CONV_W - dlt, cols]
            acc = acc + shifted[(dlt - 1) * L:dlt * L, :] * w
            head = head + jnp.where(row8 < dlt, pltpu.roll(tail, dlt, axis=0), 0.0) * w
        acc = jnp.concatenate([acc[0:8, :] + head, acc[8:, :]], axis=0)
        tail_ref[:, cols] = uf[L - 8:L, :]
        return acc * jax.nn.sigmoid(acc)

    zi = zif_ref[...]
    logf = jnp.minimum(zi, 0.0) - jnp.log1p(jnp.exp(-jnp.abs(zi)))
    row = lax.broadcasted_iota(jnp.int32, (L, L), 0)
    col = lax.broadcasted_iota(jnp.int32, (L, L), 1)
    causal = row >= col
    bcs = jnp.dot(causal.astype(F32), logf, preferred_element_type=F32,
                  precision=lax.Precision.HIGHEST)
    zi_t = zi.T
    bcs_t = bcs.T

    for hd in range(N_HEADS_M):
        cols = slice(hd * HEAD_DIM_M, (hd + 1) * HEAD_DIM_M)
        q = conv_silu(uq_ref, tq_ref, cwq_ref, cbq_ref, cols)
        k = conv_silu(uk_ref, tk_ref, cwk_ref, cbk_ref, cols) * (HEAD_DIM_M ** -0.5)
        qb = q.astype(BF16)
        kb = k.astype(BF16)
        v = v_ref[:, cols]
        i_col = zi[:, hd:hd + 1]
        b_col = bcs[:, N_HEADS_M + hd:N_HEADS_M + hd + 1]
        i_row = zi_t[hd:hd + 1, :]
        b_row = bcs_t[N_HEADS_M + hd:N_HEADS_M + hd + 1, :]
        m_prev = m_ref[hd, 0:1, 0:1]
        c_prev = c_ref[hd]
        n_prev = n_ref[hd]

        d = jnp.where(causal, b_col + (i_row - b_row), NEG)
        m_inter = b_col + m_prev
        m_t = jnp.maximum(m_inter, jnp.max(d, axis=-1, keepdims=True))
        qk = lax.dot_general(qb, kb, (((1,), (1,)), ((), ())), preferred_element_type=F32)
        sm = jnp.exp(d - m_t) * qk
        w_inter = jnp.exp(m_inter - m_t)
        num = (jnp.dot(sm.astype(BF16), v, preferred_element_type=F32)
               + w_inter * jnp.dot(qb, c_prev.astype(BF16), preferred_element_type=F32))
        den = (jnp.sum(sm, axis=-1, keepdims=True)
               + w_inter * jnp.sum(q * n_prev, axis=-1, keepdims=True))
        hh = num / jnp.maximum(jnp.abs(den), jnp.exp(-m_t))

        b_last = b_col[L - 1:L, :]
        w_end = b_last - b_col + i_col
        g_end = jnp.max(w_end, axis=0, keepdims=True)
        m_new = jnp.maximum(b_last + m_prev, g_end)
        decay = jnp.exp(b_last + m_prev - m_new)
        inj = jnp.exp(g_end - m_new)
        kw = k * jnp.exp(w_end - g_end)
        a_c = lax.dot_general(kw.astype(BF16), v, (((0,), (0,)), ((), ())), preferred_element_type=F32)
        c_ref[hd] = decay * c_prev + inj * a_c
        n_ref[hd] = decay * n_prev + inj * jnp.sum(kw, axis=0, keepdims=True)
        m_ref[hd] = jnp.broadcast_to(m_new, m_ref.shape[1:])

        hn = hh * lax.rsqrt(jnp.mean(hh * hh, axis=-1, keepdims=True) + EPS) * g_ref[:, cols]
        out_ref[:, cols] = (o_ref[:, cols].astype(F32) * hn).astype(BF16)


def _row_shift_matrices():
    t = np.arange(ML_L)
    blocks = [(t[None, :] == t[:, None] - d) for d in range(1, CONV_W)]
    return jnp.asarray(np.concatenate(blocks, axis=0), BF16)


def _mlstm(plain3, sig3, zif3, conv_w, conv_b, m_g):
    b, s, _ = plain3.shape
    L = ML_L
    dm = D_MLSTM
    return pl.pallas_call(
        _mlstm_kernel,
        out_shape=jax.ShapeDtypeStruct((b, s, dm), BF16),
        grid=(b, s // L),
        in_specs=[pl.BlockSpec((None, L, dm), lambda bi, ci: (bi, ci, PLAIN_QM // dm)),
                  pl.BlockSpec((None, L, dm), lambda bi, ci: (bi, ci, PLAIN_KM // dm)),
                  pl.BlockSpec((None, L, dm), lambda bi, ci: (bi, ci, PLAIN_VM // dm)),
                  pl.BlockSpec((None, L, dm), lambda bi, ci: (bi, ci, SIG_OM // dm)),
                  pl.BlockSpec((None, L, IF_COLS), lambda bi, ci: (bi, ci, 0)),
                  pl.BlockSpec((CONV_W, dm), lambda bi, ci: (0, 0)),
                  pl.BlockSpec((CONV_W, dm), lambda bi, ci: (0, 1)),
                  pl.BlockSpec((1, dm), lambda bi, ci: (0, 0)),
                  pl.BlockSpec((1, dm), lambda bi, ci: (0, 1)),
                  pl.BlockSpec((1, dm), lambda bi, ci: (0, 0)),
                  pl.BlockSpec(((CONV_W - 1) * L, L), lambda bi, ci: (0, 0))],
        out_specs=pl.BlockSpec((None, L, dm), lambda bi, ci: (bi, ci, 0)),
        scratch_shapes=[pltpu.VMEM((N_HEADS_M, HEAD_DIM_M, HEAD_DIM_M), F32),
                        pltpu.VMEM((N_HEADS_M, 1, HEAD_DIM_M), F32),
                        pltpu.VMEM((N_HEADS_M, 8, 128), F32),
                        pltpu.VMEM((8, dm), F32),
                        pltpu.VMEM((8, dm), F32)],
        compiler_params=_params(("parallel", "arbitrary")),
        name="mlstm",
    )(plain3, plain3, plain3, sig3, zif3, conv_w, conv_w, conv_b, conv_b, m_g, _row_shift_matrices())


def _merge_kernel(x_ref, mod_ref, a_ref, hm_ref, ga_ref, gm_ref, wa_ref, wm_ref, wo_ref, o_ref, *, row0):
    a = a_ref[...]
    hm = hm_ref[...]
    acc = None
    for j in range(D_MODEL // MERGE_TN):
        cols = slice(j * MERGE_TN, (j + 1) * MERGE_TN)
        ua = jnp.dot(a, wa_ref[:, cols], preferred_element_type=F32)
        um = jnp.dot(hm, wm_ref[:, cols], preferred_element_type=F32)
        merged = ga_ref[:, cols].astype(F32) * ua + gm_ref[:, cols].astype(F32) * um
        part = jnp.dot(merged.astype(BF16), wo_ref[cols, :], preferred_element_type=F32)
        acc = part if acc is None else acc + part
    o_ref[...] = x_ref[...] + mod_ref[row0 + 2:row0 + 3, :] * acc


def _merge(x2, mod, attn2, hm2, sig2, w_up_a, w_up_m, w_out, layer, row0, seq):
    t, d = x2.shape
    tpb = seq // MERGE_TM
    resident = pl.Buffered(1)
    return pl.pallas_call(
        functools.partial(_merge_kernel, row0=row0),
        out_shape=jax.ShapeDtypeStruct((t, d), F32),
        grid=(t // MERGE_TM,),
        in_specs=[pl.BlockSpec((MERGE_TM, d), lambda i: (i, 0)),
                  pl.BlockSpec((None, 9, d), lambda i: (i // tpb, 0, 0)),
                  pl.BlockSpec((MERGE_TM, D_ATTN), lambda i: (i, 0)),
                  pl.BlockSpec((MERGE_TM, D_MLSTM), lambda i: (i, 0)),
                  pl.BlockSpec((MERGE_TM, d), lambda i: (i, SIG_GA // d)),
                  pl.BlockSpec((MERGE_TM, d), lambda i: (i, SIG_GM // d)),
                  pl.BlockSpec((None, D_ATTN, d), lambda i: (layer, 0, 0), pipeline_mode=resident),
                  pl.BlockSpec((None, D_MLSTM, d), lambda i: (layer, 0, 0), pipeline_mode=resident),
                  pl.BlockSpec((None, d, d), lambda i: (layer, 0, 0), pipeline_mode=resident)],
        out_specs=pl.BlockSpec((MERGE_TM, d), lambda i: (i, 0)),
        compiler_params=_params(("parallel",), VMEM_LIMIT_MERGE),
        name="merge",
    )(x2, mod, attn2, hm2, sig2, sig2, w_up_a, w_up_m, w_out)


def kernel(x, c, norm_g, w_ada, b_ada, ffn1_w1, ffn1_w3, ffn1_w2, w_in, b_if, conv_w, conv_b, q_norm_g, k_norm_g,
           rel_table, m_norm_g, w_up_a, w_up_m, w_out, ffn2_w1, ffn2_w3, ffn2_w2):
    bsz, seq, d = x.shape
    depth = w_ada.shape[0]
    t = bsz * seq
    assert d == D_MODEL and seq % FFN_TM == 0 and seq % ML_L == 0 and seq >= ATT_W and bsz <= 8

    c8 = jnp.zeros((8, d), F32).at[:bsz].set(c)
    mod_all = _ada(c8, w_ada, b_ada)[:, :bsz].reshape(depth, bsz, 9, d)

    f1 = (_cast_pad(ffn1_w1, d, D_FF_PAD, d, CAST_B), _cast_pad(ffn1_w3, d, D_FF_PAD, d, CAST_B),
          _cast_pad(ffn1_w2, D_FF_PAD, d, CAST_B, d))
    f2 = (_cast_pad(ffn2_w1, d, D_FF_PAD, d, CAST_B), _cast_pad(ffn2_w3, d, D_FF_PAD, d, CAST_B),
          _cast_pad(ffn2_w2, D_FF_PAD, d, CAST_B, d))
    w_p, w_if = _cast_win(w_in)
    wa = _cast_pad(w_up_a, D_ATTN, d, D_ATTN, d)
    wm = _cast_pad(w_up_m, D_MLSTM, d, D_MLSTM, d)
    wo = _cast_pad(w_out, d, d, d // 2, d)

    x2 = x.reshape(t, d)
    for l in range(depth):
        mod = mod_all[l]
        x2 = _ffn(x2, norm_g[l, 0:1], mod, *f1, l, 0, seq)

        bif = jnp.pad(b_if[l], (0, IF_COLS - N_IF)).reshape(1, IF_COLS)
        qk_gain = jnp.stack([q_norm_g[l] * HEAD_DIM_A ** -0.5, k_norm_g[l]]).reshape(2, 1, HEAD_DIM_A)
        sig2, norm2, plain2, zif2 = _proj(x2, norm_g[l, 1:2], mod, w_p, w_if, bif, qk_gain, l, 3, seq)
        plain3 = plain2.reshape(bsz, seq, PLAIN_COLS)
        attn = _attention(norm2.reshape(bsz, seq, NORM_COLS), plain3, _attn_bias(rel_table[l]))
        hm = _mlstm(plain3, sig2.reshape(bsz, seq, SIG_COLS), zif2.reshape(bsz, seq, IF_COLS), conv_w[l],
                    conv_b[l].reshape(1, -1), m_norm_g[l].reshape(1, -1))
        x2 = _merge(x2, mod, attn.reshape(t, D_ATTN), hm.reshape(t, D_MLSTM), sig2, wa, wm, wo, l, 3, seq)

        x2 = _ffn(x2, norm_g[l, 2:3], mod, *f2, l, 6, seq)
    return x2.reshape(bsz, seq, d)
```

```python
import functools

import numpy as np
import jax
import jax.numpy as jnp
from jax import lax
from jax.experimental import pallas as pl
from jax.experimental.pallas import tpu as pltpu

F32 = jnp.float32
BF16 = jnp.bfloat16

D_MODEL = 2048
CHUNK = 64
N_HEADS_A = 8
HEAD_DIM_A = 128
D_ATTN = N_HEADS_A * HEAD_DIM_A
LEFT_CHUNKS = 8
LEFT = LEFT_CHUNKS * CHUNK
MAX_REL = 4 * CHUNK
N_HEADS_M = 4
HEAD_DIM_M = 256
D_MLSTM = N_HEADS_M * HEAD_DIM_M
CONV_W = 4
D_FF = ((8 * D_MODEL // 3 + 127) // 128) * 128
EPS = 1e-6
NEG = -1e30

SIG_GA, SIG_GM, SIG_OM = 0, D_MODEL, 2 * D_MODEL
SIG_COLS = D_MLSTM + 2 * D_MODEL
NORM_QA, NORM_KA = 0, D_ATTN
NORM_COLS = 2 * D_ATTN
PLAIN_VA, PLAIN_QM, PLAIN_KM, PLAIN_VM = 0, D_ATTN, D_ATTN + D_MLSTM, D_ATTN + 2 * D_MLSTM
PLAIN_COLS = D_ATTN + 3 * D_MLSTM
WP_SIG, WP_NORM, WP_PLAIN = 0, SIG_COLS, SIG_COLS + NORM_COLS
Z_COLS = SIG_COLS + NORM_COLS + PLAIN_COLS
IF_COLS = 128
N_IF = 2 * N_HEADS_M
WIN_OM = 3 * D_ATTN + 3 * D_MLSTM
IF0 = WIN_OM + D_MLSTM

FFN_TM, FFN_TF = 1024, 512
D_FF_PAD = ((D_FF + FFN_TF - 1) // FFN_TF) * FFN_TF
PROJ_TM = 512
PROJ_STEPS = 4
PROJ_SIG_W, PROJ_NORM_W, PROJ_PLAIN_W = SIG_COLS // PROJ_STEPS, NORM_COLS // PROJ_STEPS, PLAIN_COLS // PROJ_STEPS
ATT_TQ = 256
ATT_W = LEFT + ATT_TQ
ATT_NVAR = LEFT // ATT_TQ + 1
ATT_ROLL = 1024
ATT_HG = 4
CHUNK_LOG2 = 6
assert ATT_W + ATT_TQ - 1 <= ATT_ROLL and 1 << CHUNK_LOG2 == CHUNK
ML_L = 256
MERGE_TM, MERGE_TN = 512, 512
ADA_TN = 1024
WIN_TN = 1024
WIN_J_GATE0, WIN_J_GATE1 = SIG_GA // WIN_TN, SIG_OM // WIN_TN
CAST_B = 512
MIB = 1024 * 1024
VMEM_LIMIT = 48 * MIB
VMEM_LIMIT_MERGE = 58 * MIB
VMEM_LIMIT_FFN = 60 * MIB
MOD_ROWS = 16
MOD_UNROLL = 8


def _params(sem, vmem_limit=VMEM_LIMIT):
    return pltpu.CompilerParams(dimension_semantics=sem, vmem_limit_bytes=vmem_limit)


def _modulate_into(h_ref, x_ref, g_ref, mod_ref, row0):
    x = x_ref[...]
    y = x * lax.rsqrt(jnp.mean(x * x, axis=-1, keepdims=True) + EPS)
    h = (y * g_ref[...]) * (1.0 + mod_ref[row0 + 1:row0 + 2, :]) + mod_ref[row0:row0 + 1, :]
    h_ref[...] = h.astype(BF16)


def _modulate_rows_into(h_ref, x_ref, g_ref, mod_ref, row0, p_ref):
    d = x_ref.shape[1]
    p_ref[0] = jnp.broadcast_to(g_ref[...], (MOD_ROWS, d))
    p_ref[1] = jnp.broadcast_to(1.0 + mod_ref[row0 + 1:row0 + 2, :], (MOD_ROWS, d))
    p_ref[2] = jnp.broadcast_to(mod_ref[row0:row0 + 1, :], (MOD_ROWS, d))

    def group(r, carry):
        rows = pl.ds(pl.multiple_of(r * MOD_ROWS, MOD_ROWS), MOD_ROWS)
        x = x_ref[rows, :]
        y = x * lax.rsqrt(jnp.mean(x * x, axis=-1, keepdims=True) + EPS)
        h = (y * p_ref[0]) * p_ref[1] + p_ref[2]
        h_ref[rows, :] = h.astype(BF16)
        return carry

    lax.fori_loop(0, x_ref.shape[0] // MOD_ROWS, group, 0, unroll=MOD_UNROLL)


def _ada_kernel(c_ref, w_ref, b_ref, o_ref):
    c = c_ref[...]
    ca = (c * jax.nn.sigmoid(c)).astype(BF16)
    o_ref[...] = jnp.dot(ca, w_ref[...].astype(BF16), preferred_element_type=F32) + b_ref[...]


def _ada(c8, w_ada, b_ada):
    depth, d, n = w_ada.shape
    return pl.pallas_call(
        _ada_kernel,
        out_shape=jax.ShapeDtypeStruct((depth, 8, n), F32),
        grid=(depth, n // ADA_TN),
        in_specs=[pl.BlockSpec((8, d), lambda l, j: (0, 0)),
                  pl.BlockSpec((None, d, ADA_TN), lambda l, j: (l, 0, j)),
                  pl.BlockSpec((None, 1, ADA_TN), lambda l, j: (l, 0, j))],
        out_specs=pl.BlockSpec((None, 8, ADA_TN), lambda l, j: (l, 0, j)),
        compiler_params=_params(("parallel", "parallel")),
        name="adaln",
    )(c8, w_ada, b_ada.reshape(depth, 1, n))


def _cast_kernel(w_ref, o_ref, *, rows, cols):
    br, bc = o_ref.shape
    x = w_ref[...]
    if rows % br or cols % bc:
        r = lax.broadcasted_iota(jnp.int32, (br, bc), 0) + pl.program_id(1) * br
        c = lax.broadcasted_iota(jnp.int32, (br, bc), 1) + pl.program_id(2) * bc
        x = jnp.where(jnp.logical_and(r < rows, c < cols), x, 0.0)
    o_ref[...] = x.astype(BF16)


def _cast_pad(w, rows_out, cols_out, br, bc, layer=None):
    depth, rows, cols = w.shape
    first, count = (0, depth) if layer is None else (layer, 1)
    return pl.pallas_call(
        functools.partial(_cast_kernel, rows=rows, cols=cols),
        out_shape=jax.ShapeDtypeStruct((count, rows_out, cols_out), BF16),
        grid=(count, rows_out // br, cols_out // bc),
        in_specs=[pl.BlockSpec((None, br, bc), lambda l, i, j: (first + l, i, j))],
        out_specs=pl.BlockSpec((None, br, bc), lambda l, i, j: (l, i, j)),
        compiler_params=_params(("parallel", "parallel", "parallel")),
        name="cast",
    )(w)


def _cast_win_kernel(a_ref, b_ref, wp_ref, wif_ref):
    j = pl.program_id(1)
    gates = jnp.logical_and(j >= WIN_J_GATE0, j < WIN_J_GATE1)

    @pl.when(jnp.logical_not(gates))
    def _():
        wp_ref[...] = a_ref[...].astype(BF16)

    @pl.when(gates)
    def _():
        full = jnp.concatenate([a_ref[...], b_ref[...]], axis=1)
        wp_ref[...] = pltpu.roll(full, full.shape[1] - N_IF, axis=1)[:, :WIN_TN].astype(BF16)

    @pl.when(j == WIN_J_GATE0)
    def _():
        lane = lax.broadcasted_iota(jnp.int32, wif_ref.shape, 1)
        wif_ref[...] = jnp.where(lane < N_IF, a_ref[:, :IF_COLS], 0.0).astype(BF16)


def _win_block(j):
    j_om = WIN_OM // WIN_TN
    return jnp.where(j < WIN_J_GATE1, j + j_om + 1, jnp.where(j == WIN_J_GATE1, j_om, j - (WIN_J_GATE1 + 1)))


def _cast_win(w_in):
    depth, d, _ = w_in.shape
    per = WIN_TN // IF_COLS

    def next_block(l, j):
        jg = jnp.clip(j, WIN_J_GATE0, WIN_J_GATE1 - 1)
        return (l, 0, (_win_block(jg) + 1) * per)

    return pl.pallas_call(
        _cast_win_kernel,
        out_shape=(jax.ShapeDtypeStruct((depth, d, Z_COLS), BF16), jax.ShapeDtypeStruct((depth, d, IF_COLS), BF16)),
        grid=(depth, Z_COLS // WIN_TN),
        in_specs=[pl.BlockSpec((None, d, WIN_TN), lambda l, j: (l, 0, _win_block(j))),
                  pl.BlockSpec((None, d, IF_COLS), next_block)],
        out_specs=(pl.BlockSpec((None, d, WIN_TN), lambda l, j: (l, 0, j)),
                   pl.BlockSpec((None, d, IF_COLS), lambda l, j: (l, 0, 0))),
        compiler_params=_params(("parallel", "arbitrary")),
        name="cast_win",
    )(w_in, w_in)


def _ffn_kernel(x_ref, g_ref, mod_ref, w1_ref, w3_ref, w2_ref, *rest, row0, d_ff_next):
    if d_ff_next is None:
        o_ref, h_ref, p_ref = rest
    else:
        n1_ref, n3_ref, n2_ref, o_ref, c1_ref, c3_ref, c2_ref, h_ref, p_ref = rest
    j = pl.program_id(1)

    @pl.when(j == 0)
    def _():
        _modulate_rows_into(h_ref, x_ref, g_ref, mod_ref, row0, p_ref)

    h = h_ref[...]
    a = jnp.dot(h, w1_ref[...], preferred_element_type=F32)
    b = jnp.dot(h, w3_ref[...], preferred_element_type=F32)
    if d_ff_next is not None:
        tf = c1_ref.shape[1]
        hid = j * tf + lax.broadcasted_iota(jnp.int32, (1, tf), 1)
        c1_ref[...] = jnp.where(hid < d_ff_next, n1_ref[...], 0.0).astype(BF16)
        c3_ref[...] = jnp.where(hid < d_ff_next, n3_ref[...], 0.0).astype(BF16)
        hid = j * tf + lax.broadcasted_iota(jnp.int32, (tf, 1), 0)
        c2_ref[...] = jnp.where(hid < d_ff_next, n2_ref[...], 0.0).astype(BF16)
    u = (a * jax.nn.sigmoid(a)) * b
    prev = jnp.where(j == 0, 0.0, o_ref[...])
    o_ref[...] = prev + jnp.dot(u.astype(BF16), w2_ref[...], preferred_element_type=F32)

    @pl.when(j == pl.num_programs(1) - 1)
    def _():
        o_ref[...] = x_ref[...] + (0.5 * mod_ref[row0 + 2:row0 + 3, :]) * o_ref[...]


def _ffn(x2, g, mod, w1, w3, w2, row0, seq, nxt=None):
    t, d = x2.shape
    tpb = seq // FFN_TM
    ni = t // FFN_TM
    nf = w1.shape[2] // FFN_TF
    in_specs = [pl.BlockSpec((FFN_TM, d), lambda i, j: (i, 0)),
                pl.BlockSpec((1, d), lambda i, j: (0, 0)),
                pl.BlockSpec((None, 9, d), lambda i, j: (i // tpb, 0, 0)),
                pl.BlockSpec((None, d, FFN_TF), lambda i, j: (0, 0, j)),
                pl.BlockSpec((None, d, FFN_TF), lambda i, j: (0, 0, j)),
                pl.BlockSpec((None, FFN_TF, d), lambda i, j: (0, j, 0))]
    out_shape = [jax.ShapeDtypeStruct((t, d), F32)]
    out_specs = [pl.BlockSpec((FFN_TM, d), lambda i, j: (i, 0))]
    args = [x2, g, mod, w1, w3, w2]
    d_ff_next = None
    if nxt is not None:
        n1, n3, n2, lyr = nxt
        d_ff_next = n1.shape[2]
        assert d % ni == 0 and pl.cdiv(d_ff_next, FFN_TF) == nf
        cr = d // ni
        in_specs += [pl.BlockSpec((None, cr, FFN_TF), lambda i, j: (lyr, i, j)),
                     pl.BlockSpec((None, cr, FFN_TF), lambda i, j: (lyr, i, j)),
                     pl.BlockSpec((None, FFN_TF, cr), lambda i, j: (lyr, j, i))]
        out_shape += [jax.ShapeDtypeStruct((1, d, nf * FFN_TF), BF16), jax.ShapeDtypeStruct((1, d, nf * FFN_TF), BF16),
                      jax.ShapeDtypeStruct((1, nf * FFN_TF, d), BF16)]
        out_specs += [pl.BlockSpec((None, cr, FFN_TF), lambda i, j: (0, i, j)),
                      pl.BlockSpec((None, cr, FFN_TF), lambda i, j: (0, i, j)),
                      pl.BlockSpec((None, FFN_TF, cr), lambda i, j: (0, j, i))]
        args += [n1, n3, n2]
    res = pl.pallas_call(
        functools.partial(_ffn_kernel, row0=row0, d_ff_next=d_ff_next),
        out_shape=tuple(out_shape),
        grid=(ni, nf),
        in_specs=in_specs,
        out_specs=tuple(out_specs),
        scratch_shapes=[pltpu.VMEM((FFN_TM, d), BF16), pltpu.VMEM((3, MOD_ROWS, d), F32)],
        compiler_params=_params(("parallel", "arbitrary"), VMEM_LIMIT_FFN),
        name="ffn",
    )(*args)
    return res[0], tuple(res[1:])


def _proj_kernel(x_ref, g_ref, mod_ref, ws_ref, wn_ref, wp_ref, wif_ref, bif_ref, gain_ref,
                 sig_ref, norm_ref, plain_ref, zif_ref, h_ref, *, row0):
    @pl.when(pl.program_id(1) == 0)
    def _():
        _modulate_into(h_ref, x_ref, g_ref, mod_ref, row0)
        zif_ref[...] = jnp.dot(h_ref[...], wif_ref[...], preferred_element_type=F32) + bif_ref[...]

    h = h_ref[...]
    sig_ref[...] = jax.nn.sigmoid(jnp.dot(h, ws_ref[...], preferred_element_type=F32)).astype(BF16)
    qk = jnp.dot(h, wn_ref[...], preferred_element_type=F32)
    for hd in range(PROJ_NORM_W // HEAD_DIM_A):
        cols = slice(hd * HEAD_DIM_A, (hd + 1) * HEAD_DIM_A)
        a = qk[:, cols]
        y = a * lax.rsqrt(jnp.mean(a * a, axis=-1, keepdims=True) + EPS)
        norm_ref[:, cols] = (y * gain_ref[...]).astype(BF16)
    plain_ref[...] = jnp.dot(h, wp_ref[...], preferred_element_type=F32).astype(BF16)


def _proj(x2, g, mod, w_p, w_if, b_if, qk_gain, layer, row0, seq):
    t, d = x2.shape
    tpb = seq // PROJ_TM
    sw, nw, pw = PROJ_SIG_W, PROJ_NORM_W, PROJ_PLAIN_W
    q_steps = D_ATTN // nw
    return pl.pallas_call(
        functools.partial(_proj_kernel, row0=row0),
        out_shape=(jax.ShapeDtypeStruct((t, SIG_COLS), BF16), jax.ShapeDtypeStruct((t, NORM_COLS), BF16),
                   jax.ShapeDtypeStruct((t, PLAIN_COLS), BF16), jax.ShapeDtypeStruct((t, IF_COLS), F32)),
        grid=(t // PROJ_TM, PROJ_STEPS),
        in_specs=[pl.BlockSpec((PROJ_TM, d), lambda i, j: (i, 0)),
                  pl.BlockSpec((1, d), lambda i, j: (0, 0)),
                  pl.BlockSpec((None, 9, d), lambda i, j: (i // tpb, 0, 0)),
                  pl.BlockSpec((None, d, sw), lambda i, j: (layer, 0, WP_SIG // sw + j)),
                  pl.BlockSpec((None, d, nw), lambda i, j: (layer, 0, WP_NORM // nw + j)),
                  pl.BlockSpec((None, d, pw), lambda i, j: (layer, 0, WP_PLAIN // pw + j)),
                  pl.BlockSpec((None, d, IF_COLS), lambda i, j: (layer, 0, 0)),
                  pl.BlockSpec((1, IF_COLS), lambda i, j: (0, 0)),
                  pl.BlockSpec((None, 1, HEAD_DIM_A), lambda i, j: (j // q_steps, 0, 0))],
        out_specs=(pl.BlockSpec((PROJ_TM, sw), lambda i, j: (i, j)),
                   pl.BlockSpec((PROJ_TM, nw), lambda i, j: (i, j)),
                   pl.BlockSpec((PROJ_TM, pw), lambda i, j: (i, j)),
                   pl.BlockSpec((PROJ_TM, IF_COLS), lambda i, j: (i, 0))),
        scratch_shapes=[pltpu.VMEM((PROJ_TM, d), BF16)],
        compiler_params=_params(("parallel", "arbitrary")),
        name="proj",
    )(x2, g, mod, w_p, w_p, w_p, w_if, b_if, qk_gain)


def _bias_kernel(g_ref, o_ref):
    r = lax.broadcasted_iota(jnp.int32, (ATT_TQ, ATT_W), 0)
    c = lax.broadcasted_iota(jnp.int32, (ATT_TQ, ATT_W), 1)
    for v in range(ATT_NVAR):
        x = jnp.broadcast_to(g_ref[v:v + 1, :], (ATT_TQ, ATT_ROLL))
        y = pltpu.roll(x, 0, 1, stride=1, stride_axis=0)[:, :ATT_W]
        dq = jnp.right_shift(r + v * ATT_TQ, CHUNK_LOG2) - jnp.right_shift(c, CHUNK_LOG2)
        valid = jnp.logical_and(dq >= 0, dq <= LEFT_CHUNKS)
        o_ref[v] = jnp.where(valid, y, NEG)


def _attn_bias(table):
    h = table.shape[0]
    p = np.arange(ATT_ROLL)
    u = np.where(p < ATT_W, p, p - ATT_ROLL)
    idx = np.stack([np.clip(v * ATT_TQ - u, -(CHUNK - 1), MAX_REL) + (CHUNK - 1) for v in range(ATT_NVAR)])
    g = jnp.take(table, jnp.asarray(idx.reshape(-1), jnp.int32), axis=1).reshape(h, ATT_NVAR, ATT_ROLL)
    return pl.pallas_call(
        _bias_kernel,
        out_shape=jax.ShapeDtypeStruct((h, ATT_NVAR, ATT_TQ, ATT_W), F32),
        grid=(h,),
        in_specs=[pl.BlockSpec((None, ATT_NVAR, ATT_ROLL), lambda i: (i, 0, 0))],
        out_specs=pl.BlockSpec((None, ATT_NVAR, ATT_TQ, ATT_W), lambda i: (i, 0, 0, 0)),
        compiler_params=_params(("parallel",)),
        name="attn_bias",
    )(g)


def _attn_kernel(q_ref, k_ref, v_ref, bias_ref, o_ref):
    qi = pl.program_id(2)
    var = jnp.minimum(qi, ATT_NVAR - 1)
    ks = pl.multiple_of((qi - var) * ATT_TQ, ATT_TQ)

    def scores(hd):
        cols = slice(hd * HEAD_DIM_A, (hd + 1) * HEAD_DIM_A)
        q = q_ref[:, cols]
        k = k_ref[pl.ds(ks, ATT_W), cols]
        s = lax.dot_general(q, k, (((1,), (1,)), ((), ())), preferred_element_type=F32)
        return s + bias_ref[hd, var]

    s_next = scores(0)
    for hd in range(ATT_HG):
        cols = slice(hd * HEAD_DIM_A, (hd + 1) * HEAD_DIM_A)
        s = s_next
        if hd + 1 < ATT_HG:
            s_next = scores(hd + 1)
        p = jnp.exp(s - jnp.max(s, axis=-1, keepdims=True))
        l = jnp.sum(p, axis=-1, keepdims=True)
        v = v_ref[pl.ds(ks, ATT_W), cols]
        o = jnp.dot(p.astype(BF16), v, preferred_element_type=F32)
        o_ref[:, cols] = (o / l).astype(BF16)


def _attention(norm3, plain3, bias):
    b, s, _ = norm3.shape
    gw = ATT_HG * HEAD_DIM_A
    nvar = bias.shape[1]
    return pl.pallas_call(
        _attn_kernel,
        out_shape=jax.ShapeDtypeStruct((b, s, D_ATTN), BF16),
        grid=(N_HEADS_A // ATT_HG, b, s // ATT_TQ),
        in_specs=[pl.BlockSpec((None, ATT_TQ, gw), lambda g, bi, qi: (bi, qi, NORM_QA // gw + g)),
                  pl.BlockSpec((None, s, gw), lambda g, bi, qi: (bi, 0, NORM_KA // gw + g)),
                  pl.BlockSpec((None, s, gw), lambda g, bi, qi: (bi, 0, PLAIN_VA // gw + g)),
                  pl.BlockSpec((ATT_HG, nvar, ATT_TQ, ATT_W), lambda g, bi, qi: (g, 0, 0, 0))],
        out_specs=pl.BlockSpec((None, ATT_TQ, gw), lambda g, bi, qi: (bi, qi, g)),
        compiler_params=_params(("parallel", "parallel", "arbitrary")),
        name="attn",
    )(norm3, norm3, plain3, bias)


def _mlstm_kernel(uq_ref, uk_ref, v_ref, o_ref, zif_ref, cwq_ref, cwk_ref, cbq_ref, cbk_ref, g_ref, shift_ref, out_ref,
                  c_ref, n_ref, m_ref, tq_ref, tk_ref):
    L = ML_L
    ci = pl.program_id(1)

    @pl.when(ci == 0)
    def _():
        c_ref[...] = jnp.zeros_like(c_ref)
        n_ref[...] = jnp.zeros_like(n_ref)
        m_ref[...] = jnp.zeros_like(m_ref)
        tq_ref[...] = jnp.zeros_like(tq_ref)
        tk_ref[...] = jnp.zeros_like(tk_ref)

    row8 = lax.broadcasted_iota(jnp.int32, (8, HEAD_DIM_M), 0)

    def conv_silu(u_ref, tail_ref, cw_ref, cb_ref, cols):
        u = u_ref[:, cols]
        uf = u.astype(F32)
        shifted = jnp.dot(shift_ref[...], u, preferred_element_type=F32)
        acc = cb_ref[:, cols] + uf * cw_ref[CONV_W - 1:CONV_W, cols]
        tail = tail_ref[:, cols]
        head = jnp.zeros((8, HEAD_DIM_M), F32)
        for dlt in range(1, CONV_W):
            w = cw_ref[CONV_W - 1 - dlt:CONV_W - dlt, cols]
            acc = acc + shifted[(dlt - 1) * L:dlt * L, :] * w
            head = head + jnp.where(row8 < dlt, pltpu.roll(tail, dlt, axis=0), 0.0) * w
        acc = jnp.concatenate([acc[0:8, :] + head, acc[8:, :]], axis=0)
        tail_ref[:, cols] = uf[L - 8:L, :]
        return acc * jax.nn.sigmoid(acc)

    zi = zif_ref[...]
    logf = jnp.minimum(zi, 0.0) - jnp.log1p(jnp.exp(-jnp.abs(zi)))
    row = lax.broadcasted_iota(jnp.int32, (L, L), 0)
    col = lax.broadcasted_iota(jnp.int32, (L, L), 1)
    causal = row >= col
    bcs = jnp.dot(causal.astype(F32), logf, preferred_element_type=F32,
                  precision=lax.Precision.HIGHEST)
    zi_t = zi.T
    bcs_t = bcs.T

    for hd in range(N_HEADS_M):
        cols = slice(hd * HEAD_DIM_M, (hd + 1) * HEAD_DIM_M)
        q = conv_silu(uq_ref, tq_ref, cwq_ref, cbq_ref, cols)
        k = conv_silu(uk_ref, tk_ref, cwk_ref, cbk_ref, cols) * (HEAD_DIM_M ** -0.5)
        qb = q.astype(BF16)
        kb = k.astype(BF16)
        v = v_ref[:, cols]
        i_col = zi[:, hd:hd + 1]
        b_col = bcs[:, N_HEADS_M + hd:N_HEADS_M + hd + 1]
        i_row = zi_t[hd:hd + 1, :]
        b_row = bcs_t[N_HEADS_M + hd:N_HEADS_M + hd + 1, :]
        m_prev = m_ref[hd, 0:1, 0:1]
        c_prev = c_ref[hd]
        n_prev = n_ref[hd]

        d = jnp.where(causal, b_col + (i_row - b_row), NEG)
        m_inter = b_col + m_prev
        m_t = jnp.maximum(m_inter, jnp.max(d, axis=-1, keepdims=True))
        qk = lax.dot_general(qb, kb, (((1,), (1,)), ((), ())), preferred_element_type=F32)
        sm = jnp.exp(d - m_t) * qk
        w_inter = jnp.exp(m_inter - m_t)
        num = (jnp.dot(sm.astype(BF16), v, preferred_element_type=F32)
               + w_inter * jnp.dot(qb, c_prev.astype(BF16), preferred_element_type=F32))
        den = (jnp.sum(sm, axis=-1, keepdims=True)
               + w_inter * jnp.sum(q * n_prev, axis=-1, keepdims=True))
        hh = num / jnp.maximum(jnp.abs(den), jnp.exp(-m_t))

        b_last = b_col[L - 1:L, :]
        w_end = b_last - b_col + i_col
        g_end = jnp.max(w_end, axis=0, keepdims=True)
        m_new = jnp.maximum(b_last + m_prev, g_end)
        decay = jnp.exp(b_last + m_prev - m_new)
        inj = jnp.exp(g_end - m_new)
        kw = k * jnp.exp(w_end - g_end)
        a_c = lax.dot_general(kw.astype(BF16), v, (((0,), (0,)), ((), ())), preferred_element_type=F32)
        c_ref[hd] = decay * c_prev + inj * a_c
        n_ref[hd] = decay * n_prev + inj * jnp.sum(kw, axis=0, keepdims=True)
        m_ref[hd] = jnp.broadcast_to(m_new, m_ref.shape[1:])

        hn = hh * lax.rsqrt(jnp.mean(hh * hh, axis=-1, keepdims=True) + EPS) * g_ref[:, cols]
        out_ref[:, cols] = (o_ref[:, cols].astype(F32) * hn).astype(BF16)


def _row_shift_matrices():
    t = np.arange(ML_L)
    blocks = [(t[None, :] == t[:, None] - d) for d in range(1, CONV_W)]
    return jnp.asarray(np.concatenate(blocks, axis=0), BF16)


def _mlstm(plain3, sig3, zif3, conv_w, conv_b, m_g):
    b, s, _ = plain3.shape
    L = ML_L
    dm = D_MLSTM
    return pl.pallas_call(
        _mlstm_kernel,
        out_shape=jax.ShapeDtypeStruct((b, s, dm), BF16),
        grid=(b, s // L),
        in_specs=[pl.BlockSpec((None, L, dm), lambda bi, ci: (bi, ci, PLAIN_QM // dm)),
                  pl.BlockSpec((None, L, dm), lambda bi, ci: (bi, ci, PLAIN_KM // dm)),
                  pl.BlockSpec((None, L, dm), lambda bi, ci: (bi, ci, PLAIN_VM // dm)),
                  pl.BlockSpec((None, L, dm), lambda bi, ci: (bi, ci, SIG_OM // dm)),
                  pl.BlockSpec((None, L, IF_COLS), lambda bi, ci: (bi, ci, 0)),
                  pl.BlockSpec((CONV_W, dm), lambda bi, ci: (0, 0)),
                  pl.BlockSpec((CONV_W, dm), lambda bi, ci: (0, 1)),
                  pl.BlockSpec((1, dm), lambda bi, ci: (0, 0)),
                  pl.BlockSpec((1, dm), lambda bi, ci: (0, 1)),
                  pl.BlockSpec((1, dm), lambda bi, ci: (0, 0)),
                  pl.BlockSpec(((CONV_W - 1) * L, L), lambda bi, ci: (0, 0))],
        out_specs=pl.BlockSpec((None, L, dm), lambda bi, ci: (bi, ci, 0)),
        scratch_shapes=[pltpu.VMEM((N_HEADS_M, HEAD_DIM_M, HEAD_DIM_M), F32),
                        pltpu.VMEM((N_HEADS_M, 1, HEAD_DIM_M), F32),
                        pltpu.VMEM((N_HEADS_M, 8, 128), F32),
                        pltpu.VMEM((8, dm), F32),
                        pltpu.VMEM((8, dm), F32)],
        compiler_params=_params(("parallel", "arbitrary")),
        name="mlstm",
    )(plain3, plain3, plain3, sig3, zif3, conv_w, conv_w, conv_b, conv_b, m_g, _row_shift_matrices())


def _merge_kernel(x_ref, mod_ref, a_ref, hm_ref, ga_ref, gm_ref, wa_ref, wm_ref, wo_ref, o_ref, *, row0):
    a = a_ref[...]
    hm = hm_ref[...]
    acc = None
    for j in range(D_MODEL // MERGE_TN):
        cols = slice(j * MERGE_TN, (j + 1) * MERGE_TN)
        ua = jnp.dot(a, wa_ref[:, cols], preferred_element_type=F32)
        um = jnp.dot(hm, wm_ref[:, cols], preferred_element_type=F32)
        merged = ga_ref[:, cols].astype(F32) * ua + gm_ref[:, cols].astype(F32) * um
        part = jnp.dot(merged.astype(BF16), wo_ref[cols, :], preferred_element_type=F32)
        acc = part if acc is None else acc + part
    o_ref[...] = x_ref[...] + mod_ref[row0 + 2:row0 + 3, :] * acc


def _merge(x2, mod, attn2, hm2, sig2, w_up_a, w_up_m, w_out, layer, row0, seq):
    t, d = x2.shape
    tpb = seq // MERGE_TM
    resident = pl.Buffered(1)
    return pl.pallas_call(
        functools.partial(_merge_kernel, row0=row0),
        out_shape=jax.ShapeDtypeStruct((t, d), F32),
        grid=(t // MERGE_TM,),
        in_specs=[pl.BlockSpec((MERGE_TM, d), lambda i: (i, 0)),
                  pl.BlockSpec((None, 9, d), lambda i: (i // tpb, 0, 0)),
                  pl.BlockSpec((MERGE_TM, D_ATTN), lambda i: (i, 0)),
                  pl.BlockSpec((MERGE_TM, D_MLSTM), lambda i: (i, 0)),
                  pl.BlockSpec((MERGE_TM, d), lambda i: (i, SIG_GA // d)),
                  pl.BlockSpec((MERGE_TM, d), lambda i: (i, SIG_GM // d)),
                  pl.BlockSpec((None, D_ATTN, d), lambda i: (layer, 0, 0), pipeline_mode=resident),
                  pl.BlockSpec((None, D_MLSTM, d), lambda i: (layer, 0, 0), pipeline_mode=resident),
                  pl.BlockSpec((None, d, d), lambda i: (layer, 0, 0), pipeline_mode=resident)],
        out_specs=pl.BlockSpec((MERGE_TM, d), lambda i: (i, 0)),
        compiler_params=_params(("parallel",), VMEM_LIMIT_MERGE),
        name="merge",
    )(x2, mod, attn2, hm2, sig2, sig2, w_up_a, w_up_m, w_out)


def kernel(x, c, norm_g, w_ada, b_ada, ffn1_w1, ffn1_w3, ffn1_w2, w_in, b_if, conv_w, conv_b, q_norm_g, k_norm_g,
           rel_table, m_norm_g, w_up_a, w_up_m, w_out, ffn2_w1, ffn2_w3, ffn2_w2):
    bsz, seq, d = x.shape
    depth = w_ada.shape[0]
    t = bsz * seq
    assert d == D_MODEL and seq % FFN_TM == 0 and seq % ML_L == 0 and seq >= ATT_W and bsz <= 8

    c8 = jnp.zeros((8, d), F32).at[:bsz].set(c)
    mod_all = _ada(c8, w_ada, b_ada)[:, :bsz].reshape(depth, bsz, 9, d)

    fw = (_cast_pad(ffn1_w1, d, D_FF_PAD, d, CAST_B, layer=0), _cast_pad(ffn1_w3, d, D_FF_PAD, d, CAST_B, layer=0),
          _cast_pad(ffn1_w2, D_FF_PAD, d, CAST_B, d, layer=0))
    w_p, w_if = _cast_win(w_in)
    wa = _cast_pad(w_up_a, D_ATTN, d, D_ATTN, d)
    wm = _cast_pad(w_up_m, D_MLSTM, d, D_MLSTM, d)
    wo = _cast_pad(w_out, d, d, d // 2, d)

    x2 = x.reshape(t, d)
    for l in range(depth):
        mod = mod_all[l]
        x2, fw = _ffn(x2, norm_g[l, 0:1], mod, *fw, 0, seq, nxt=(ffn2_w1, ffn2_w3, ffn2_w2, l))

        bif = jnp.pad(b_if[l], (0, IF_COLS - N_IF)).reshape(1, IF_COLS)
        qk_gain = jnp.stack([q_norm_g[l] * HEAD_DIM_A ** -0.5, k_norm_g[l]]).reshape(2, 1, HEAD_DIM_A)
        sig2, norm2, plain2, zif2 = _proj(x2, norm_g[l, 1:2], mod, w_p, w_if, bif, qk_gain, l, 3, seq)
        plain3 = plain2.reshape(bsz, seq, PLAIN_COLS)
        attn = _attention(norm2.reshape(bsz, seq, NORM_COLS), plain3, _attn_bias(rel_table[l]))
        hm = _mlstm(plain3, sig2.reshape(bsz, seq, SIG_COLS), zif2.reshape(bsz, seq, IF_COLS), conv_w[l],
                    conv_b[l].reshape(1, -1), m_norm_g[l].reshape(1, -1))
        x2 = _merge(x2, mod, attn.reshape(t, D_ATTN), hm.reshape(t, D_MLSTM), sig2, wa, wm, wo, l, 3, seq)

        nxt = (ffn1_w1, ffn1_w3, ffn1_w2, l + 1) if l + 1 < depth else None
        x2, fw = _ffn(x2, norm_g[l, 2:3], mod, *fw, 6, seq, nxt=nxt)
    return x2.reshape(bsz, seq, d)
```

```python
import functools

import numpy as np
import jax
import jax.numpy as jnp
from jax import lax
from jax.experimental import pallas as pl
from jax.experimental.pallas import tpu as pltpu

F32 = jnp.float32
BF16 = jnp.bfloat16

D_MODEL = 2048
CHUNK = 64
N_HEADS_A = 8
HEAD_DIM_A = 128
D_ATTN = N_HEADS_A * HEAD_DIM_A
LEFT_CHUNKS = 8
LEFT = LEFT_CHUNKS * CHUNK
MAX_REL = 4 * CHUNK
N_HEADS_M = 4
HEAD_DIM_M = 256
D_MLSTM = N_HEADS_M * HEAD_DIM_M
CONV_W = 4
D_FF = ((8 * D_MODEL // 3 + 127) // 128) * 128
EPS = 1e-6
NEG = -1e30

SIG_GA, SIG_GM, SIG_OM = 0, D_MODEL, 2 * D_MODEL
SIG_COLS = D_MLSTM + 2 * D_MODEL
NORM_QA, NORM_KA = 0, D_ATTN
NORM_COLS = 2 * D_ATTN
PLAIN_VA, PLAIN_QM, PLAIN_KM, PLAIN_VM = 0, D_ATTN, D_ATTN + D_MLSTM, D_ATTN + 2 * D_MLSTM
PLAIN_COLS = D_ATTN + 3 * D_MLSTM
WP_SIG, WP_NORM, WP_PLAIN = 0, SIG_COLS, SIG_COLS + NORM_COLS
Z_COLS = SIG_COLS + NORM_COLS + PLAIN_COLS
IF_COLS = 128
N_IF = 2 * N_HEADS_M
WIN_OM = 3 * D_ATTN + 3 * D_MLSTM
IF0 = WIN_OM + D_MLSTM

FFN_TM, FFN_TF = 1024, 512
D_FF_PAD = ((D_FF + FFN_TF - 1) // FFN_TF) * FFN_TF
PROJ_TM = 512
PROJ_STEPS = 4
PROJ_SIG_W, PROJ_NORM_W, PROJ_PLAIN_W = SIG_COLS // PROJ_STEPS, NORM_COLS // PROJ_STEPS, PLAIN_COLS // PROJ_STEPS
ATT_TQ = 256
ATT_W = LEFT + ATT_TQ
ATT_NVAR = LEFT // ATT_TQ + 1
ATT_ROLL = 1024
ATT_HG = 4
ATT_SUB = 4
CHUNK_LOG2 = 6
assert ATT_W + ATT_TQ - 1 <= ATT_ROLL and 1 << CHUNK_LOG2 == CHUNK
ML_L = 256
MERGE_TM, MERGE_TN = 512, 512
ADA_TN = 1024
WIN_TN = 1024
WIN_J_GATE0, WIN_J_GATE1 = SIG_GA // WIN_TN, SIG_OM // WIN_TN
CAST_B = 512
MIB = 1024 * 1024
VMEM_LIMIT = 48 * MIB
VMEM_LIMIT_MERGE = 58 * MIB
VMEM_LIMIT_ATTN = 56 * MIB
VMEM_LIMIT_FFN = 60 * MIB
MOD_ROWS = 16
MOD_UNROLL = 8


def _params(sem, vmem_limit=VMEM_LIMIT):
    return pltpu.CompilerParams(dimension_semantics=sem, vmem_limit_bytes=vmem_limit)


def _modulate_into(h_ref, x_ref, g_ref, mod_ref, row0):
    x = x_ref[...]
    y = x * lax.rsqrt(jnp.mean(x * x, axis=-1, keepdims=True) + EPS)
    h = (y * g_ref[...]) * (1.0 + mod_ref[row0 + 1:row0 + 2, :]) + mod_ref[row0:row0 + 1, :]
    h_ref[...] = h.astype(BF16)


def _modulate_rows_into(h_ref, x_ref, g_ref, mod_ref, row0, p_ref):
    d = x_ref.shape[1]
    p_ref[0] = jnp.broadcast_to(g_ref[...], (MOD_ROWS, d))
    p_ref[1] = jnp.broadcast_to(1.0 + mod_ref[row0 + 1:row0 + 2, :], (MOD_ROWS, d))
    p_ref[2] = jnp.broadcast_to(mod_ref[row0:row0 + 1, :], (MOD_ROWS, d))

    def group(r, carry):
        rows = pl.ds(pl.multiple_of(r * MOD_ROWS, MOD_ROWS), MOD_ROWS)
        x = x_ref[rows, :]
        y = x * lax.rsqrt(jnp.mean(x * x, axis=-1, keepdims=True) + EPS)
        h = (y * p_ref[0]) * p_ref[1] + p_ref[2]
        h_ref[rows, :] = h.astype(BF16)
        return carry

    lax.fori_loop(0, x_ref.shape[0] // MOD_ROWS, group, 0, unroll=MOD_UNROLL)


def _ada_kernel(c_ref, w_ref, b_ref, o_ref):
    c = c_ref[...]
    ca = (c * jax.nn.sigmoid(c)).astype(BF16)
    o_ref[...] = jnp.dot(ca, w_ref[...].astype(BF16), preferred_element_type=F32) + b_ref[...]


def _ada(c8, w_ada, b_ada):
    depth, d, n = w_ada.shape
    return pl.pallas_call(
        _ada_kernel,
        out_shape=jax.ShapeDtypeStruct((depth, 8, n), F32),
        grid=(depth, n // ADA_TN),
        in_specs=[pl.BlockSpec((8, d), lambda l, j: (0, 0)),
                  pl.BlockSpec((None, d, ADA_TN), lambda l, j: (l, 0, j)),
                  pl.BlockSpec((None, 1, ADA_TN), lambda l, j: (l, 0, j))],
        out_specs=pl.BlockSpec((None, 8, ADA_TN), lambda l, j: (l, 0, j)),
        compiler_params=_params(("parallel", "parallel")),
        name="adaln",
    )(c8, w_ada, b_ada.reshape(depth, 1, n))


def _cast_kernel(w_ref, o_ref, *, rows, cols):
    br, bc = o_ref.shape
    x = w_ref[...]
    if rows % br or cols % bc:
        r = lax.broadcasted_iota(jnp.int32, (br, bc), 0) + pl.program_id(1) * br
        c = lax.broadcasted_iota(jnp.int32, (br, bc), 1) + pl.program_id(2) * bc
        x = jnp.where(jnp.logical_and(r < rows, c < cols), x, 0.0)
    o_ref[...] = x.astype(BF16)


def _cast_pad(w, rows_out, cols_out, br, bc, layer=None):
    depth, rows, cols = w.shape
    first, count = (0, depth) if layer is None else (layer, 1)
    return pl.pallas_call(
        functools.partial(_cast_kernel, rows=rows, cols=cols),
        out_shape=jax.ShapeDtypeStruct((count, rows_out, cols_out), BF16),
        grid=(count, rows_out // br, cols_out // bc),
        in_specs=[pl.BlockSpec((None, br, bc), lambda l, i, j: (first + l, i, j))],
        out_specs=pl.BlockSpec((None, br, bc), lambda l, i, j: (l, i, j)),
        compiler_params=_params(("parallel", "parallel", "parallel")),
        name="cast",
    )(w)


def _cast_win_kernel(a_ref, b_ref, wp_ref, wif_ref):
    j = pl.program_id(1)
    gates = jnp.logical_and(j >= WIN_J_GATE0, j < WIN_J_GATE1)

    @pl.when(jnp.logical_not(gates))
    def _():
        wp_ref[...] = a_ref[...].astype(BF16)

    @pl.when(gates)
    def _():
        full = jnp.concatenate([a_ref[...], b_ref[...]], axis=1)
        wp_ref[...] = pltpu.roll(full, full.shape[1] - N_IF, axis=1)[:, :WIN_TN].astype(BF16)

    @pl.when(j == WIN_J_GATE0)
    def _():
        lane = lax.broadcasted_iota(jnp.int32, wif_ref.shape, 1)
        wif_ref[...] = jnp.where(lane < N_IF, a_ref[:, :IF_COLS], 0.0).astype(BF16)


def _win_block(j):
    j_om = WIN_OM // WIN_TN
    return jnp.where(j < WIN_J_GATE1, j + j_om + 1, jnp.where(j == WIN_J_GATE1, j_om, j - (WIN_J_GATE1 + 1)))


def _cast_win(w_in):
    depth, d, _ = w_in.shape
    per = WIN_TN // IF_COLS

    def next_block(l, j):
        jg = jnp.clip(j, WIN_J_GATE0, WIN_J_GATE1 - 1)
        return (l, 0, (_win_block(jg) + 1) * per)

    return pl.pallas_call(
        _cast_win_kernel,
        out_shape=(jax.ShapeDtypeStruct((depth, d, Z_COLS), BF16), jax.ShapeDtypeStruct((depth, d, IF_COLS), BF16)),
        grid=(depth, Z_COLS // WIN_TN),
        in_specs=[pl.BlockSpec((None, d, WIN_TN), lambda l, j: (l, 0, _win_block(j))),
                  pl.BlockSpec((None, d, IF_COLS), next_block)],
        out_specs=(pl.BlockSpec((None, d, WIN_TN), lambda l, j: (l, 0, j)),
                   pl.BlockSpec((None, d, IF_COLS), lambda l, j: (l, 0, 0))),
        compiler_params=_params(("parallel", "arbitrary")),
        name="cast_win",
    )(w_in, w_in)


def _ffn_kernel(x_ref, g_ref, mod_ref, w1_ref, w3_ref, w2_ref, *rest, row0, d_ff_next):
    if d_ff_next is None:
        o_ref, h_ref, p_ref = rest
    else:
        n1_ref, n3_ref, n2_ref, o_ref, c1_ref, c3_ref, c2_ref, h_ref, p_ref = rest
    j = pl.program_id(1)

    @pl.when(j == 0)
    def _():
        _modulate_rows_into(h_ref, x_ref, g_ref, mod_ref, row0, p_ref)

    h = h_ref[...]
    a = jnp.dot(h, w1_ref[...], preferred_element_type=F32)
    b = jnp.dot(h, w3_ref[...], preferred_element_type=F32)
    if d_ff_next is not None:
        tf = c1_ref.shape[1]
        hid = j * tf + lax.broadcasted_iota(jnp.int32, (1, tf), 1)
        c1_ref[...] = jnp.where(hid < d_ff_next, n1_ref[...], 0.0).astype(BF16)
        c3_ref[...] = jnp.where(hid < d_ff_next, n3_ref[...], 0.0).astype(BF16)
        hid = j * tf + lax.broadcasted_iota(jnp.int32, (tf, 1), 0)
        c2_ref[...] = jnp.where(hid < d_ff_next, n2_ref[...], 0.0).astype(BF16)
    u = (a * jax.nn.sigmoid(a)) * b
    prev = jnp.where(j == 0, 0.0, o_ref[...])
    o_ref[...] = prev + jnp.dot(u.astype(BF16), w2_ref[...], preferred_element_type=F32)

    @pl.when(j == pl.num_programs(1) - 1)
    def _():
        o_ref[...] = x_ref[...] + (0.5 * mod_ref[row0 + 2:row0 + 3, :]) * o_ref[...]


def _ffn(x2, g, mod, w1, w3, w2, row0, seq, nxt=None):
    t, d = x2.shape
    tpb = seq // FFN_TM
    ni = t // FFN_TM
    nf = w1.shape[2] // FFN_TF
    in_specs = [pl.BlockSpec((FFN_TM, d), lambda i, j: (i, 0)),
                pl.BlockSpec((1, d), lambda i, j: (0, 0)),
                pl.BlockSpec((None, 9, d), lambda i, j: (i // tpb, 0, 0)),
                pl.BlockSpec((None, d, FFN_TF), lambda i, j: (0, 0, j)),
                pl.BlockSpec((None, d, FFN_TF), lambda i, j: (0, 0, j)),
                pl.BlockSpec((None, FFN_TF, d), lambda i, j: (0, j, 0))]
    out_shape = [jax.ShapeDtypeStruct((t, d), F32)]
    out_specs = [pl.BlockSpec((FFN_TM, d), lambda i, j: (i, 0))]
    args = [x2, g, mod, w1, w3, w2]
    d_ff_next = None
    if nxt is not None:
        n1, n3, n2, lyr = nxt
        d_ff_next = n1.shape[2]
        assert d % ni == 0 and pl.cdiv(d_ff_next, FFN_TF) == nf
        cr = d // ni
        in_specs += [pl.BlockSpec((None, cr, FFN_TF), lambda i, j: (lyr, i, j)),
                     pl.BlockSpec((None, cr, FFN_TF), lambda i, j: (lyr, i, j)),
                     pl.BlockSpec((None, FFN_TF, cr), lambda i, j: (lyr, j, i))]
        out_shape += [jax.ShapeDtypeStruct((1, d, nf * FFN_TF), BF16), jax.ShapeDtypeStruct((1, d, nf * FFN_TF), BF16),
                      jax.ShapeDtypeStruct((1, nf * FFN_TF, d), BF16)]
        out_specs += [pl.BlockSpec((None, cr, FFN_TF), lambda i, j: (0, i, j)),
                      pl.BlockSpec((None, cr, FFN_TF), lambda i, j: (0, i, j)),
                      pl.BlockSpec((None, FFN_TF, cr), lambda i, j: (0, j, i))]
        args += [n1, n3, n2]
    res = pl.pallas_call(
        functools.partial(_ffn_kernel, row0=row0, d_ff_next=d_ff_next),
        out_shape=tuple(out_shape),
        grid=(ni, nf),
        in_specs=in_specs,
        out_specs=tuple(out_specs),
        scratch_shapes=[pltpu.VMEM((FFN_TM, d), BF16), pltpu.VMEM((3, MOD_ROWS, d), F32)],
        compiler_params=_params(("parallel", "arbitrary"), VMEM_LIMIT_FFN),
        name="ffn",
    )(*args)
    return res[0], tuple(res[1:])


def _proj_kernel(x_ref, g_ref, mod_ref, ws_ref, wn_ref, wp_ref, wif_ref, bif_ref, gain_ref,
                 sig_ref, norm_ref, plain_ref, zif_ref, h_ref, *, row0):
    @pl.when(pl.program_id(1) == 0)
    def _():
        _modulate_into(h_ref, x_ref, g_ref, mod_ref, row0)
        zif_ref[...] = jnp.dot(h_ref[...], wif_ref[...], preferred_element_type=F32) + bif_ref[...]

    h = h_ref[...]
    sig_ref[...] = jax.nn.sigmoid(jnp.dot(h, ws_ref[...], preferred_element_type=F32)).astype(BF16)
    qk = jnp.dot(h, wn_ref[...], preferred_element_type=F32)
    for hd in range(PROJ_NORM_W // HEAD_DIM_A):
        cols = slice(hd * HEAD_DIM_A, (hd + 1) * HEAD_DIM_A)
        a = qk[:, cols]
        y = a * lax.rsqrt(jnp.mean(a * a, axis=-1, keepdims=True) + EPS)
        norm_ref[:, cols] = (y * gain_ref[...]).astype(BF16)
    plain_ref[...] = jnp.dot(h, wp_ref[...], preferred_element_type=F32).astype(BF16)


def _proj(x2, g, mod, w_p, w_if, b_if, qk_gain, layer, row0, seq):
    t, d = x2.shape
    tpb = seq // PROJ_TM
    sw, nw, pw = PROJ_SIG_W, PROJ_NORM_W, PROJ_PLAIN_W
    q_steps = D_ATTN // nw
    return pl.pallas_call(
        functools.partial(_proj_kernel, row0=row0),
        out_shape=(jax.ShapeDtypeStruct((t, SIG_COLS), BF16), jax.ShapeDtypeStruct((t, NORM_COLS), BF16),
                   jax.ShapeDtypeStruct((t, PLAIN_COLS), BF16), jax.ShapeDtypeStruct((t, IF_COLS), F32)),
        grid=(t // PROJ_TM, PROJ_STEPS),
        in_specs=[pl.BlockSpec((PROJ_TM, d), lambda i, j: (i, 0)),
                  pl.BlockSpec((1, d), lambda i, j: (0, 0)),
                  pl.BlockSpec((None, 9, d), lambda i, j: (i // tpb, 0, 0)),
                  pl.BlockSpec((None, d, sw), lambda i, j: (layer, 0, WP_SIG // sw + j)),
                  pl.BlockSpec((None, d, nw), lambda i, j: (layer, 0, WP_NORM // nw + j)),
                  pl.BlockSpec((None, d, pw), lambda i, j: (layer, 0, WP_PLAIN // pw + j)),
                  pl.BlockSpec((None, d, IF_COLS), lambda i, j: (layer, 0, 0)),
                  pl.BlockSpec((1, IF_COLS), lambda i, j: (0, 0)),
                  pl.BlockSpec((None, 1, HEAD_DIM_A), lambda i, j: (j // q_steps, 0, 0))],
        out_specs=(pl.BlockSpec((PROJ_TM, sw), lambda i, j: (i, j)),
                   pl.BlockSpec((PROJ_TM, nw), lambda i, j: (i, j)),
                   pl.BlockSpec((PROJ_TM, pw), lambda i, j: (i, j)),
                   pl.BlockSpec((PROJ_TM, IF_COLS), lambda i, j: (i, 0))),
        scratch_shapes=[pltpu.VMEM((PROJ_TM, d), BF16)],
        compiler_params=_params(("parallel", "arbitrary")),
        name="proj",
    )(x2, g, mod, w_p, w_p, w_p, w_if, b_if, qk_gain)


def _bias_kernel(g_ref, o_ref):
    r = lax.broadcasted_iota(jnp.int32, (ATT_TQ, ATT_W), 0)
    c = lax.broadcasted_iota(jnp.int32, (ATT_TQ, ATT_W), 1)
    for v in range(ATT_NVAR):
        x = jnp.broadcast_to(g_ref[v:v + 1, :], (ATT_TQ, ATT_ROLL))
        y = pltpu.roll(x, 0, 1, stride=1, stride_axis=0)[:, :ATT_W]
        dq = jnp.right_shift(r + v * ATT_TQ, CHUNK_LOG2) - jnp.right_shift(c, CHUNK_LOG2)
        valid = jnp.logical_and(dq >= 0, dq <= LEFT_CHUNKS)
        o_ref[v] = jnp.where(valid, y, NEG)


def _attn_bias(table):
    h = table.shape[0]
    p = np.arange(ATT_ROLL)
    u = np.where(p < ATT_W, p, p - ATT_ROLL)
    idx = np.stack([np.clip(v * ATT_TQ - u, -(CHUNK - 1), MAX_REL) + (CHUNK - 1) for v in range(ATT_NVAR)])
    g = jnp.take(table, jnp.asarray(idx.reshape(-1), jnp.int32), axis=1).reshape(h, ATT_NVAR, ATT_ROLL)
    return pl.pallas_call(
        _bias_kernel,
        out_shape=jax.ShapeDtypeStruct((h, ATT_NVAR, ATT_TQ, ATT_W), F32),
        grid=(h,),
        in_specs=[pl.BlockSpec((None, ATT_NVAR, ATT_ROLL), lambda i: (i, 0, 0))],
        out_specs=pl.BlockSpec((None, ATT_NVAR, ATT_TQ, ATT_W), lambda i: (i, 0, 0, 0)),
        compiler_params=_params(("parallel",)),
        name="attn_bias",
    )(g)


def _attn_kernel(q_ref, k_ref, v_ref, bias_ref, o_ref):
    units = [(sub, hd) for sub in range(ATT_SUB) for hd in range(ATT_HG)]

    def window(sub):
        qi = pl.program_id(2) * ATT_SUB + sub
        var = jnp.minimum(qi, ATT_NVAR - 1)
        return var, pl.multiple_of((qi - var) * ATT_TQ, ATT_TQ)

    def scores(sub, hd):
        var, ks = window(sub)
        cols = slice(hd * HEAD_DIM_A, (hd + 1) * HEAD_DIM_A)
        q = q_ref[sub * ATT_TQ:(sub + 1) * ATT_TQ, cols]
        k = k_ref[pl.ds(ks, ATT_W), cols]
        s = lax.dot_general(q, k, (((1,), (1,)), ((), ())), preferred_element_type=F32)
        return s + bias_ref[hd, var]

    s_next = scores(*units[0])
    for n, (sub, hd) in enumerate(units):
        cols = slice(hd * HEAD_DIM_A, (hd + 1) * HEAD_DIM_A)
        s = s_next
        if n + 1 < len(units):
            s_next = scores(*units[n + 1])
        p = jnp.exp(s - jnp.max(s, axis=-1, keepdims=True))
        l = jnp.sum(p, axis=-1, keepdims=True)
        _, ks = window(sub)
        v = v_ref[pl.ds(ks, ATT_W), cols]
        o = jnp.dot(p.astype(BF16), v, preferred_element_type=F32)
        o_ref[sub * ATT_TQ:(sub + 1) * ATT_TQ, cols] = (o / l).astype(BF16)


def _attention(norm3, plain3, bias):
    b, s, _ = norm3.shape
    gw = ATT_HG * HEAD_DIM_A
    tq = ATT_TQ * ATT_SUB
    nvar = bias.shape[1]
    return pl.pallas_call(
        _attn_kernel,
        out_shape=jax.ShapeDtypeStruct((b, s, D_ATTN), BF16),
        grid=(N_HEADS_A // ATT_HG, b, s // tq),
        in_specs=[pl.BlockSpec((None, tq, gw), lambda g, bi, qi: (bi, qi, NORM_QA // gw + g)),
                  pl.BlockSpec((None, s, gw), lambda g, bi, qi: (bi, 0, NORM_KA // gw + g)),
                  pl.BlockSpec((None, s, gw), lambda g, bi, qi: (bi, 0, PLAIN_VA // gw + g)),
                  pl.BlockSpec((ATT_HG, nvar, ATT_TQ, ATT_W), lambda g, bi, qi: (g, 0, 0, 0))],
        out_specs=pl.BlockSpec((None, tq, gw), lambda g, bi, qi: (bi, qi, g)),
        compiler_params=_params(("parallel", "parallel", "arbitrary"), VMEM_LIMIT_ATTN),
        name="attn",
    )(norm3, norm3, plain3, bias)


def _mlstm_kernel(uq_ref, uk_ref, v_ref, o_ref, zif_ref, cwq_ref, cwk_ref, cbq_ref, cbk_ref, g_ref, shift_ref, out_ref,
                  c_ref, n_ref, m_ref, tq_ref, tk_ref):
    L = ML_L
    ci = pl.program_id(1)

    @pl.when(ci == 0)
    def _():
        c_ref[...] = jnp.zeros_like(c_ref)
        n_ref[...] = jnp.zeros_like(n_ref)
        m_ref[...] = jnp.zeros_like(m_ref)
        tq_ref[...] = jnp.zeros_like(tq_ref)
        tk_ref[...] = jnp.zeros_like(tk_ref)

    row8 = lax.broadcasted_iota(jnp.int32, (8, HEAD_DIM_M), 0)

    def conv_silu(u_ref, tail_ref, cw_ref, cb_ref, cols):
        u = u_ref[:, cols]
        uf = u.astype(F32)
        shifted = jnp.dot(shift_ref[...], u, preferred_element_type=F32)
        acc = cb_ref[:, cols] + uf * cw_ref[CONV_W - 1:CONV_W, cols]
        tail = tail_ref[:, cols]
        head = jnp.zeros((8, HEAD_DIM_M), F32)
        for dlt in range(1, CONV_W):
            w = cw_ref[CONV_W - 1 - dlt:CONV_W - dlt, cols]
            acc = acc + shifted[(dlt - 1) * L:dlt * L, :] * w
            head = head + jnp.where(row8 < dlt, pltpu.roll(tail, dlt, axis=0), 0.0) * w
        acc = jnp.concatenate([acc[0:8, :] + head, acc[8:, :]], axis=0)
        tail_ref[:, cols] = uf[L - 8:L, :]
        return acc * jax.nn.sigmoid(acc)

    zi = zif_ref[...]
    logf = jnp.minimum(zi, 0.0) - jnp.log1p(jnp.exp(-jnp.abs(zi)))
    row = lax.broadcasted_iota(jnp.int32, (L, L), 0)
    col = lax.broadcasted_iota(jnp.int32, (L, L), 1)
    causal = row >= col
    bcs = jnp.dot(causal.astype(F32), logf, preferred_element_type=F32,
                  precision=lax.Precision.HIGHEST)
    zi_t = zi.T
    bcs_t = bcs.T

    for hd in range(N_HEADS_M):
        cols = slice(hd * HEAD_DIM_M, (hd + 1) * HEAD_DIM_M)
        q = conv_silu(uq_ref, tq_ref, cwq_ref, cbq_ref, cols)
        k = conv_silu(uk_ref, tk_ref, cwk_ref, cbk_ref, cols) * (HEAD_DIM_M ** -0.5)
        qb = q.astype(BF16)
        kb = k.astype(BF16)
        v = v_ref[:, cols]
        i_col = zi[:, hd:hd + 1]
        b_col = bcs[:, N_HEADS_M + hd:N_HEADS_M + hd + 1]
        i_row = zi_t[hd:hd + 1, :]
        b_row = bcs_t[N_HEADS_M + hd:N_HEADS_M + hd + 1, :]
        m_prev = m_ref[hd, 0:1, 0:1]
        c_prev = c_ref[hd]
        n_prev = n_ref[hd]

        d = jnp.where(causal, b_col + (i_row - b_row), NEG)
        m_inter = b_col + m_prev
        m_t = jnp.maximum(m_inter, jnp.max(d, axis=-1, keepdims=True))
        qk = lax.dot_general(qb, kb, (((1,), (1,)), ((), ())), preferred_element_type=F32)
        sm = jnp.exp(d - m_t) * qk
        w_inter = jnp.exp(m_inter - m_t)
        num = (jnp.dot(sm.astype(BF16), v, preferred_element_type=F32)
               + w_inter * jnp.dot(qb, c_prev.astype(BF16), preferred_element_type=F32))
        den = (jnp.sum(sm, axis=-1, keepdims=True)
               + w_inter * jnp.sum(q * n_prev, axis=-1, keepdims=True))
        hh = num / jnp.maximum(jnp.abs(den), jnp.exp(-m_t))

        b_last = b_col[L - 1:L, :]
        w_end = b_last - b_col + i_col
        g_end = jnp.max(w_end, axis=0, keepdims=True)
        m_new = jnp.maximum(b_last + m_prev, g_end)
        decay = jnp.exp(b_last + m_prev - m_new)
        inj = jnp.exp(g_end - m_new)
        kw = k * jnp.exp(w_end - g_end)
        a_c = lax.dot_general(kw.astype(BF16), v, (((0,), (0,)), ((), ())), preferred_element_type=F32)
        c_ref[hd] = decay * c_prev + inj * a_c
        n_ref[hd] = decay * n_prev + inj * jnp.sum(kw, axis=0, keepdims=True)
        m_ref[hd] = jnp.broadcast_to(m_new, m_ref.shape[1:])

        hn = hh * lax.rsqrt(jnp.mean(hh * hh, axis=-1, keepdims=True) + EPS) * g_ref[:, cols]
        out_ref[:, cols] = (o_ref[:, cols].astype(F32) * hn).astype(BF16)


def _row_shift_matrices():
    t = np.arange(ML_L)
    blocks = [(t[None, :] == t[:, None] - d) for d in range(1, CONV_W)]
    return jnp.asarray(np.concatenate(blocks, axis=0), BF16)


def _mlstm(plain3, sig3, zif3, conv_w, conv_b, m_g):
    b, s, _ = plain3.shape
    L = ML_L
    dm = D_MLSTM
    return pl.pallas_call(
        _mlstm_kernel,
        out_shape=jax.ShapeDtypeStruct((b, s, dm), BF16),
        grid=(b, s // L),
        in_specs=[pl.BlockSpec((None, L, dm), lambda bi, ci: (bi, ci, PLAIN_QM // dm)),
                  pl.BlockSpec((None, L, dm), lambda bi, ci: (bi, ci, PLAIN_KM // dm)),
                  pl.BlockSpec((None, L, dm), lambda bi, ci: (bi, ci, PLAIN_VM // dm)),
                  pl.BlockSpec((None, L, dm), lambda bi, ci: (bi, ci, SIG_OM // dm)),
                  pl.BlockSpec((None, L, IF_COLS), lambda bi, ci: (bi, ci, 0)),
                  pl.BlockSpec((CONV_W, dm), lambda bi, ci: (0, 0)),
                  pl.BlockSpec((CONV_W, dm), lambda bi, ci: (0, 1)),
                  pl.BlockSpec((1, dm), lambda bi, ci: (0, 0)),
                  pl.BlockSpec((1, dm), lambda bi, ci: (0, 1)),
                  pl.BlockSpec((1, dm), lambda bi, ci: (0, 0)),
                  pl.BlockSpec(((CONV_W - 1) * L, L), lambda bi, ci: (0, 0))],
        out_specs=pl.BlockSpec((None, L, dm), lambda bi, ci: (bi, ci, 0)),
        scratch_shapes=[pltpu.VMEM((N_HEADS_M, HEAD_DIM_M, HEAD_DIM_M), F32),
                        pltpu.VMEM((N_HEADS_M, 1, HEAD_DIM_M), F32),
                        pltpu.VMEM((N_HEADS_M, 8, 128), F32),
                        pltpu.VMEM((8, dm), F32),
                        pltpu.VMEM((8, dm), F32)],
        compiler_params=_params(("parallel", "arbitrary")),
        name="mlstm",
    )(plain3, plain3, plain3, sig3, zif3, conv_w, conv_w, conv_b, conv_b, m_g, _row_shift_matrices())


def _merge_kernel(x_ref, mod_ref, a_ref, hm_ref, ga_ref, gm_ref, wa_ref, wm_ref, wo_ref, o_ref, *, row0):
    a = a_ref[...]
    hm = hm_ref[...]
    acc = None
    for j in range(D_MODEL // MERGE_TN):
        cols = slice(j * MERGE_TN, (j + 1) * MERGE_TN)
        ua = jnp.dot(a, wa_ref[:, cols], preferred_element_type=F32)
        um = jnp.dot(hm, wm_ref[:, cols], preferred_element_type=F32)
        merged = ga_ref[:, cols].astype(F32) * ua + gm_ref[:, cols].astype(F32) * um
        part = jnp.dot(merged.astype(BF16), wo_ref[cols, :], preferred_element_type=F32)
        acc = part if acc is None else acc + part
    o_ref[...] = x_ref[...] + mod_ref[row0 + 2:row0 + 3, :] * acc


def _merge(x2, mod, attn2, hm2, sig2, w_up_a, w_up_m, w_out, layer, row0, seq):
    t, d = x2.shape
    tpb = seq // MERGE_TM
    resident = pl.Buffered(1)
    return pl.pallas_call(
        functools.partial(_merge_kernel, row0=row0),
        out_shape=jax.ShapeDtypeStruct((t, d), F32),
        grid=(t // MERGE_TM,),
        in_specs=[pl.BlockSpec((MERGE_TM, d), lambda i: (i, 0)),
                  pl.BlockSpec((None, 9, d), lambda i: (i // tpb, 0, 0)),
                  pl.BlockSpec((MERGE_TM, D_ATTN), lambda i: (i, 0)),
                  pl.BlockSpec((MERGE_TM, D_MLSTM), lambda i: (i, 0)),
                  pl.BlockSpec((MERGE_TM, d), lambda i: (i, SIG_GA // d)),
                  pl.BlockSpec((MERGE_TM, d), lambda i: (i, SIG_GM // d)),
                  pl.BlockSpec((None, D_ATTN, d), lambda i: (layer, 0, 0), pipeline_mode=resident),
                  pl.BlockSpec((None, D_MLSTM, d), lambda i: (layer, 0, 0), pipeline_mode=resident),
                  pl.BlockSpec((None, d, d), lambda i: (layer, 0, 0), pipeline_mode=resident)],
        out_specs=pl.BlockSpec((MERGE_TM, d), lambda i: (i, 0)),
        compiler_params=_params(("parallel",), VMEM_LIMIT_MERGE),
        name="merge",
    )(x2, mod, attn2, hm2, sig2, sig2, w_up_a, w_up_m, w_out)


def kernel(x, c, norm_g, w_ada, b_ada, ffn1_w1, ffn1_w3, ffn1_w2, w_in, b_if, conv_w, conv_b, q_norm_g, k_norm_g,
           rel_table, m_norm_g, w_up_a, w_up_m, w_out, ffn2_w1, ffn2_w3, ffn2_w2):
    bsz, seq, d = x.shape
    depth = w_ada.shape[0]
    t = bsz * seq
    assert d == D_MODEL and seq % FFN_TM == 0 and seq % ML_L == 0 and seq % (ATT_TQ * ATT_SUB) == 0
    assert seq >= ATT_W and bsz <= 8

    c8 = jnp.zeros((8, d), F32).at[:bsz].set(c)
    mod_all = _ada(c8, w_ada, b_ada)[:, :bsz].reshape(depth, bsz, 9, d)

    fw = (_cast_pad(ffn1_w1, d, D_FF_PAD, d, CAST_B, layer=0), _cast_pad(ffn1_w3, d, D_FF_PAD, d, CAST_B, layer=0),
          _cast_pad(ffn1_w2, D_FF_PAD, d, CAST_B, d, layer=0))
    w_p, w_if = _cast_win(w_in)
    wa = _cast_pad(w_up_a, D_ATTN, d, D_ATTN, d)
    wm = _cast_pad(w_up_m, D_MLSTM, d, D_MLSTM, d)
    wo = _cast_pad(w_out, d, d, d // 2, d)

    x2 = x.reshape(t, d)
    for l in range(depth):
        mod = mod_all[l]
        x2, fw = _ffn(x2, norm_g[l, 0:1], mod, *fw, 0, seq, nxt=(ffn2_w1, ffn2_w3, ffn2_w2, l))

        bif = jnp.pad(b_if[l], (0, IF_COLS - N_IF)).reshape(1, IF_COLS)
        qk_gain = jnp.stack([q_norm_g[l] * HEAD_DIM_A ** -0.5, k_norm_g[l]]).reshape(2, 1, HEAD_DIM_A)
        sig2, norm2, plain2, zif2 = _proj(x2, norm_g[l, 1:2], mod, w_p, w_if, bif, qk_gain, l, 3, seq)
        plain3 = plain2.reshape(bsz, seq, PLAIN_COLS)
        attn = _attention(norm2.reshape(bsz, seq, NORM_COLS), plain3, _attn_bias(rel_table[l]))
        hm = _mlstm(plain3, sig2.reshape(bsz, seq, SIG_COLS), zif2.reshape(bsz, seq, IF_COLS), conv_w[l],
                    conv_b[l].reshape(1, -1), m_norm_g[l].reshape(1, -1))
        x2 = _merge(x2, mod, attn.reshape(t, D_ATTN), hm.reshape(t, D_MLSTM), sig2, wa, wm, wo, l, 3, seq)

        nxt = (ffn1_w1, ffn1_w3, ffn1_w2, l + 1) if l + 1 < depth else None
        x2, fw = _ffn(x2, norm_g[l, 2:3], mod, *fw, 6, seq, nxt=nxt)
    return x2.reshape(bsz, seq, d)
```

```python
import functools

import numpy as np
import jax
import jax.numpy as jnp
from jax import lax
from jax.experimental import pallas as pl
from jax.experimental.pallas import tpu as pltpu

F32 = jnp.float32
BF16 = jnp.bfloat16

D_MODEL = 2048
CHUNK = 64
N_HEADS_A = 8
HEAD_DIM_A = 128
D_ATTN = N_HEADS_A * HEAD_DIM_A
LEFT_CHUNKS = 8
LEFT = LEFT_CHUNKS * CHUNK
MAX_REL = 4 * CHUNK
N_HEADS_M = 4
HEAD_DIM_M = 256
D_MLSTM = N_HEADS_M * HEAD_DIM_M
CONV_W = 4
D_FF = ((8 * D_MODEL // 3 + 127) // 128) * 128
EPS = 1e-6
NEG = -1e30

SIG_GA, SIG_GM, SIG_OM = 0, D_MODEL, 2 * D_MODEL
SIG_COLS = D_MLSTM + 2 * D_MODEL
NORM_QA, NORM_KA = 0, D_ATTN
NORM_COLS = 2 * D_ATTN
PLAIN_VA, PLAIN_QM, PLAIN_KM, PLAIN_VM = 0, D_ATTN, D_ATTN + D_MLSTM, D_ATTN + 2 * D_MLSTM
PLAIN_COLS = D_ATTN + 3 * D_MLSTM
WP_SIG, WP_NORM, WP_PLAIN = 0, SIG_COLS, SIG_COLS + NORM_COLS
Z_COLS = SIG_COLS + NORM_COLS + PLAIN_COLS
IF_COLS = 128
N_IF = 2 * N_HEADS_M
WIN_OM = 3 * D_ATTN + 3 * D_MLSTM
IF0 = WIN_OM + D_MLSTM

FFN_TM, FFN_TF = 1024, 512
D_FF_PAD = ((D_FF + FFN_TF - 1) // FFN_TF) * FFN_TF
PROJ_TM = 512
PROJ_STEPS = 4
PROJ_SIG_W, PROJ_NORM_W, PROJ_PLAIN_W = SIG_COLS // PROJ_STEPS, NORM_COLS // PROJ_STEPS, PLAIN_COLS // PROJ_STEPS
ATT_TQ = 256
ATT_W = LEFT + ATT_TQ
ATT_NVAR = LEFT // ATT_TQ + 1
ATT_ROLL = 1024
ATT_HG = 4
ATT_SUB = 4
CHUNK_LOG2 = 6
assert ATT_W + ATT_TQ - 1 <= ATT_ROLL and 1 << CHUNK_LOG2 == CHUNK
ML_L = 256
MERGE_TM, MERGE_TN = 512, 512
ADA_TN = 1024
WIN_TN = 1024
WIN_J_GATE0, WIN_J_GATE1 = SIG_GA // WIN_TN, SIG_OM // WIN_TN
CAST_B = 512
MIB = 1024 * 1024
VMEM_LIMIT = 48 * MIB
VMEM_LIMIT_MERGE = 58 * MIB
VMEM_LIMIT_ATTN = 56 * MIB
VMEM_LIMIT_FFN = 60 * MIB
MOD_ROWS = 16
MOD_UNROLL = 8


def _params(sem, vmem_limit=VMEM_LIMIT):
    return pltpu.CompilerParams(dimension_semantics=sem, vmem_limit_bytes=vmem_limit)


def _modulate_into(h_ref, x_ref, g_ref, mod_ref, row0):
    x = x_ref[...]
    y = x * lax.rsqrt(jnp.mean(x * x, axis=-1, keepdims=True) + EPS)
    h = (y * g_ref[...]) * (1.0 + mod_ref[row0 + 1:row0 + 2, :]) + mod_ref[row0:row0 + 1, :]
    h_ref[...] = h.astype(BF16)


def _modulate_rows_into(h_ref, x_ref, g_ref, mod_ref, row0, p_ref):
    d = x_ref.shape[1]
    p_ref[0] = jnp.broadcast_to(g_ref[...], (MOD_ROWS, d))
    p_ref[1] = jnp.broadcast_to(1.0 + mod_ref[row0 + 1:row0 + 2, :], (MOD_ROWS, d))
    p_ref[2] = jnp.broadcast_to(mod_ref[row0:row0 + 1, :], (MOD_ROWS, d))

    def group(r, carry):
        rows = pl.ds(pl.multiple_of(r * MOD_ROWS, MOD_ROWS), MOD_ROWS)
        x = x_ref[rows, :]
        y = x * lax.rsqrt(jnp.mean(x * x, axis=-1, keepdims=True) + EPS)
        h = (y * p_ref[0]) * p_ref[1] + p_ref[2]
        h_ref[rows, :] = h.astype(BF16)
        return carry

    lax.fori_loop(0, x_ref.shape[0] // MOD_ROWS, group, 0, unroll=MOD_UNROLL)


def _ada_kernel(c_ref, w_ref, b_ref, o_ref):
    c = c_ref[...]
    ca = (c * jax.nn.sigmoid(c)).astype(BF16)
    o_ref[...] = jnp.dot(ca, w_ref[...].astype(BF16), preferred_element_type=F32) + b_ref[...]


def _ada(c8, w_ada, b_ada):
    depth, d, n = w_ada.shape
    return pl.pallas_call(
        _ada_kernel,
        out_shape=jax.ShapeDtypeStruct((depth, 8, n), F32),
        grid=(depth, n // ADA_TN),
        in_specs=[pl.BlockSpec((8, d), lambda l, j: (0, 0)),
                  pl.BlockSpec((None, d, ADA_TN), lambda l, j: (l, 0, j)),
                  pl.BlockSpec((None, 1, ADA_TN), lambda l, j: (l, 0, j))],
        out_specs=pl.BlockSpec((None, 8, ADA_TN), lambda l, j: (l, 0, j)),
        compiler_params=_params(("parallel", "parallel")),
        name="adaln",
    )(c8, w_ada, b_ada.reshape(depth, 1, n))


def _cast_kernel(w_ref, o_ref, *, rows, cols):
    br, bc = o_ref.shape
    x = w_ref[...]
    if rows % br or cols % bc:
        r = lax.broadcasted_iota(jnp.int32, (br, bc), 0) + pl.program_id(1) * br
        c = lax.broadcasted_iota(jnp.int32, (br, bc), 1) + pl.program_id(2) * bc
        x = jnp.where(jnp.logical_and(r < rows, c < cols), x, 0.0)
    o_ref[...] = x.astype(BF16)


def _cast_pad(w, rows_out, cols_out, br, bc, layer=None):
    depth, rows, cols = w.shape
    first, count = (0, depth) if layer is None else (layer, 1)
    return pl.pallas_call(
        functools.partial(_cast_kernel, rows=rows, cols=cols),
        out_shape=jax.ShapeDtypeStruct((count, rows_out, cols_out), BF16),
        grid=(count, rows_out // br, cols_out // bc),
        in_specs=[pl.BlockSpec((None, br, bc), lambda l, i, j: (first + l, i, j))],
        out_specs=pl.BlockSpec((None, br, bc), lambda l, i, j: (l, i, j)),
        compiler_params=_params(("parallel", "parallel", "parallel")),
        name="cast",
    )(w)


def _cast_win_kernel(a_ref, b_ref, wp_ref, wif_ref):
    j = pl.program_id(1)
    gates = jnp.logical_and(j >= WIN_J_GATE0, j < WIN_J_GATE1)

    @pl.when(jnp.logical_not(gates))
    def _():
        wp_ref[...] = a_ref[...].T.astype(BF16)

    @pl.when(gates)
    def _():
        wp_ref[...] = jnp.concatenate([a_ref[N_IF:, :], b_ref[...]], axis=0).T.astype(BF16)

    @pl.when(j == WIN_J_GATE0)
    def _():
        row = lax.broadcasted_iota(jnp.int32, (IF_COLS, a_ref.shape[1]), 0)
        wif_ref[...] = jnp.where(row < N_IF, a_ref[0:IF_COLS, :], 0.0).T.astype(BF16)


def _win_block(j):
    j_om = WIN_OM // WIN_TN
    return jnp.where(j < WIN_J_GATE1, j + j_om + 1, jnp.where(j == WIN_J_GATE1, j_om, j - (WIN_J_GATE1 + 1)))


def _cast_win(w_in):
    depth, d, _ = w_in.shape
    w_t = jnp.swapaxes(w_in, 1, 2)
    per = WIN_TN // N_IF

    def next_block(l, j):
        jg = jnp.clip(j, WIN_J_GATE0, WIN_J_GATE1 - 1)
        return (l, (_win_block(jg) + 1) * per, 0)

    return pl.pallas_call(
        _cast_win_kernel,
        out_shape=(jax.ShapeDtypeStruct((depth, d, Z_COLS), BF16), jax.ShapeDtypeStruct((depth, d, IF_COLS), BF16)),
        grid=(depth, Z_COLS // WIN_TN),
        in_specs=[pl.BlockSpec((None, WIN_TN, d), lambda l, j: (l, _win_block(j), 0)),
                  pl.BlockSpec((None, N_IF, d), next_block)],
        out_specs=(pl.BlockSpec((None, d, WIN_TN), lambda l, j: (l, 0, j)),
                   pl.BlockSpec((None, d, IF_COLS), lambda l, j: (l, 0, 0))),
        compiler_params=_params(("parallel", "arbitrary")),
        name="cast_win",
    )(w_t, w_t)


def _ffn_kernel(x_ref, g_ref, mod_ref, w1_ref, w3_ref, w2_ref, *rest, row0, d_ff_next):
    if d_ff_next is None:
        o_ref, h_ref, p_ref = rest
    else:
        n1_ref, n3_ref, n2_ref, o_ref, c1_ref, c3_ref, c2_ref, h_ref, p_ref = rest
    j = pl.program_id(1)

    @pl.when(j == 0)
    def _():
        _modulate_rows_into(h_ref, x_ref, g_ref, mod_ref, row0, p_ref)

    h = h_ref[...]
    a = jnp.dot(h, w1_ref[...], preferred_element_type=F32)
    b = jnp.dot(h, w3_ref[...], preferred_element_type=F32)
    if d_ff_next is not None:
        tf = c1_ref.shape[1]
        hid = j * tf + lax.broadcasted_iota(jnp.int32, (1, tf), 1)
        c1_ref[...] = jnp.where(hid < d_ff_next, n1_ref[...], 0.0).astype(BF16)
        c3_ref[...] = jnp.where(hid < d_ff_next, n3_ref[...], 0.0).astype(BF16)
        hid = j * tf + lax.broadcasted_iota(jnp.int32, (tf, 1), 0)
        c2_ref[...] = jnp.where(hid < d_ff_next, n2_ref[...], 0.0).astype(BF16)
    u = (a * jax.nn.sigmoid(a)) * b
    prev = jnp.where(j == 0, 0.0, o_ref[...])
    o_ref[...] = prev + jnp.dot(u.astype(BF16), w2_ref[...], preferred_element_type=F32)

    @pl.when(j == pl.num_programs(1) - 1)
    def _():
        o_ref[...] = x_ref[...] + (0.5 * mod_ref[row0 + 2:row0 + 3, :]) * o_ref[...]


def _ffn(x2, g, mod, w1, w3, w2, row0, seq, nxt=None):
    t, d = x2.shape
    tpb = seq // FFN_TM
    ni = t // FFN_TM
    nf = w1.shape[2] // FFN_TF
    in_specs = [pl.BlockSpec((FFN_TM, d), lambda i, j: (i, 0)),
                pl.BlockSpec((1, d), lambda i, j: (0, 0)),
                pl.BlockSpec((None, 9, d), lambda i, j: (i // tpb, 0, 0)),
                pl.BlockSpec((None, d, FFN_TF), lambda i, j: (0, 0, j)),
                pl.BlockSpec((None, d, FFN_TF), lambda i, j: (0, 0, j)),
                pl.BlockSpec((None, FFN_TF, d), lambda i, j: (0, j, 0))]
    out_shape = [jax.ShapeDtypeStruct((t, d), F32)]
    out_specs = [pl.BlockSpec((FFN_TM, d), lambda i, j: (i, 0))]
    args = [x2, g, mod, w1, w3, w2]
    d_ff_next = None
    if nxt is not None:
        n1, n3, n2, lyr = nxt
        d_ff_next = n1.shape[2]
        assert d % ni == 0 and pl.cdiv(d_ff_next, FFN_TF) == nf
        cr = d // ni
        in_specs += [pl.BlockSpec((None, cr, FFN_TF), lambda i, j: (lyr, i, j)),
                     pl.BlockSpec((None, cr, FFN_TF), lambda i, j: (lyr, i, j)),
                     pl.BlockSpec((None, FFN_TF, cr), lambda i, j: (lyr, j, i))]
        out_shape += [jax.ShapeDtypeStruct((1, d, nf * FFN_TF), BF16), jax.ShapeDtypeStruct((1, d, nf * FFN_TF), BF16),
                      jax.ShapeDtypeStruct((1, nf * FFN_TF, d), BF16)]
        out_specs += [pl.BlockSpec((None, cr, FFN_TF), lambda i, j: (0, i, j)),
                      pl.BlockSpec((None, cr, FFN_TF), lambda i, j: (0, i, j)),
                      pl.BlockSpec((None, FFN_TF, cr), lambda i, j: (0, j, i))]
        args += [n1, n3, n2]
    res = pl.pallas_call(
        functools.partial(_ffn_kernel, row0=row0, d_ff_next=d_ff_next),
        out_shape=tuple(out_shape),
        grid=(ni, nf),
        in_specs=in_specs,
        out_specs=tuple(out_specs),
        scratch_shapes=[pltpu.VMEM((FFN_TM, d), BF16), pltpu.VMEM((3, MOD_ROWS, d), F32)],
        compiler_params=_params(("parallel", "arbitrary"), VMEM_LIMIT_FFN),
        name="ffn",
    )(*args)
    return res[0], tuple(res[1:])


def _proj_kernel(x_ref, g_ref, mod_ref, ws_ref, wn_ref, wp_ref, wif_ref, bif_ref, gain_ref,
                 sig_ref, norm_ref, plain_ref, zif_ref, h_ref, *, row0):
    @pl.when(pl.program_id(1) == 0)
    def _():
        _modulate_into(h_ref, x_ref, g_ref, mod_ref, row0)
        zif_ref[...] = jnp.dot(h_ref[...], wif_ref[...], preferred_element_type=F32) + bif_ref[...]

    h = h_ref[...]
    sig_ref[...] = jax.nn.sigmoid(jnp.dot(h, ws_ref[...], preferred_element_type=F32)).astype(BF16)
    qk = jnp.dot(h, wn_ref[...], preferred_element_type=F32)
    for hd in range(PROJ_NORM_W // HEAD_DIM_A):
        cols = slice(hd * HEAD_DIM_A, (hd + 1) * HEAD_DIM_A)
        a = qk[:, cols]
        y = a * lax.rsqrt(jnp.mean(a * a, axis=-1, keepdims=True) + EPS)
        norm_ref[:, cols] = (y * gain_ref[...]).astype(BF16)
    plain_ref[...] = jnp.dot(h, wp_ref[...], preferred_element_type=F32).astype(BF16)


def _proj(x2, g, mod, w_p, w_if, b_if, qk_gain, layer, row0, seq):
    t, d = x2.shape
    tpb = seq // PROJ_TM
    sw, nw, pw = PROJ_SIG_W, PROJ_NORM_W, PROJ_PLAIN_W
    q_steps = D_ATTN // nw
    return pl.pallas_call(
        functools.partial(_proj_kernel, row0=row0),
        out_shape=(jax.ShapeDtypeStruct((t, SIG_COLS), BF16), jax.ShapeDtypeStruct((t, NORM_COLS), BF16),
                   jax.ShapeDtypeStruct((t, PLAIN_COLS), BF16), jax.ShapeDtypeStruct((t, IF_COLS), F32)),
        grid=(t // PROJ_TM, PROJ_STEPS),
        in_specs=[pl.BlockSpec((PROJ_TM, d), lambda i, j: (i, 0)),
                  pl.BlockSpec((1, d), lambda i, j: (0, 0)),
                  pl.BlockSpec((None, 9, d), lambda i, j: (i // tpb, 0, 0)),
                  pl.BlockSpec((None, d, sw), lambda i, j: (layer, 0, WP_SIG // sw + j)),
                  pl.BlockSpec((None, d, nw), lambda i, j: (layer, 0, WP_NORM // nw + j)),
                  pl.BlockSpec((None, d, pw), lambda i, j: (layer, 0, WP_PLAIN // pw + j)),
                  pl.BlockSpec((None, d, IF_COLS), lambda i, j: (layer, 0, 0)),
                  pl.BlockSpec((1, IF_COLS), lambda i, j: (0, 0)),
                  pl.BlockSpec((None, 1, HEAD_DIM_A), lambda i, j: (j // q_steps, 0, 0))],
        out_specs=(pl.BlockSpec((PROJ_TM, sw), lambda i, j: (i, j)),
                   pl.BlockSpec((PROJ_TM, nw), lambda i, j: (i, j)),
                   pl.BlockSpec((PROJ_TM, pw), lambda i, j: (i, j)),
                   pl.BlockSpec((PROJ_TM, IF_COLS), lambda i, j: (i, 0))),
        scratch_shapes=[pltpu.VMEM((PROJ_TM, d), BF16)],
        compiler_params=_params(("parallel", "arbitrary")),
        name="proj",
    )(x2, g, mod, w_p, w_p, w_p, w_if, b_if, qk_gain)


def _bias_kernel(g_ref, o_ref):
    r = lax.broadcasted_iota(jnp.int32, (ATT_TQ, ATT_W), 0)
    c = lax.broadcasted_iota(jnp.int32, (ATT_TQ, ATT_W), 1)
    for v in range(ATT_NVAR):
        x = jnp.broadcast_to(g_ref[v:v + 1, :], (ATT_TQ, ATT_ROLL))
        y = pltpu.roll(x, 0, 1, stride=1, stride_axis=0)[:, :ATT_W]
        dq = jnp.right_shift(r + v * ATT_TQ, CHUNK_LOG2) - jnp.right_shift(c, CHUNK_LOG2)
        valid = jnp.logical_and(dq >= 0, dq <= LEFT_CHUNKS)
        o_ref[v] = jnp.where(valid, y, NEG)


def _attn_bias(table):
    h = table.shape[0]
    p = np.arange(ATT_ROLL)
    u = np.where(p < ATT_W, p, p - ATT_ROLL)
    idx = np.stack([np.clip(v * ATT_TQ - u, -(CHUNK - 1), MAX_REL) + (CHUNK - 1) for v in range(ATT_NVAR)])
    g = jnp.take(table, jnp.asarray(idx.reshape(-1), jnp.int32), axis=1).reshape(h, ATT_NVAR, ATT_ROLL)
    return pl.pallas_call(
        _bias_kernel,
        out_shape=jax.ShapeDtypeStruct((h, ATT_NVAR, ATT_TQ, ATT_W), F32),
        grid=(h,),
        in_specs=[pl.BlockSpec((None, ATT_NVAR, ATT_ROLL), lambda i: (i, 0, 0))],
        out_specs=pl.BlockSpec((None, ATT_NVAR, ATT_TQ, ATT_W), lambda i: (i, 0, 0, 0)),
        compiler_params=_params(("parallel",)),
        name="attn_bias",
    )(g)


def _attn_kernel(q_ref, k_ref, v_ref, bias_ref, o_ref):
    units = [(sub, hd) for sub in range(ATT_SUB) for hd in range(ATT_HG)]

    def window(sub):
        qi = pl.program_id(2) * ATT_SUB + sub
        var = jnp.minimum(qi, ATT_NVAR - 1)
        return var, pl.multiple_of((qi - var) * ATT_TQ, ATT_TQ)

    def scores(sub, hd):
        var, ks = window(sub)
        cols = slice(hd * HEAD_DIM_A, (hd + 1) * HEAD_DIM_A)
        q = q_ref[sub * ATT_TQ:(sub + 1) * ATT_TQ, cols]
        k = k_ref[pl.ds(ks, ATT_W), cols]
        s = lax.dot_general(q, k, (((1,), (1,)), ((), ())), preferred_element_type=F32)
        return s + bias_ref[hd, var]

    s_next = scores(*units[0])
    for n, (sub, hd) in enumerate(units):
        cols = slice(hd * HEAD_DIM_A, (hd + 1) * HEAD_DIM_A)
        s = s_next
        if n + 1 < len(units):
            s_next = scores(*units[n + 1])
        p = jnp.exp(s - jnp.max(s, axis=-1, keepdims=True))
        l = jnp.sum(p, axis=-1, keepdims=True)
        _, ks = window(sub)
        v = v_ref[pl.ds(ks, ATT_W), cols]
        o = jnp.dot(p.astype(BF16), v, preferred_element_type=F32)
        o_ref[sub * ATT_TQ:(sub + 1) * ATT_TQ, cols] = (o / l).astype(BF16)


def _attention(norm3, plain3, bias):
    b, s, _ = norm3.shape
    gw = ATT_HG * HEAD_DIM_A
    tq = ATT_TQ * ATT_SUB
    nvar = bias.shape[1]
    return pl.pallas_call(
        _attn_kernel,
        out_shape=jax.ShapeDtypeStruct((b, s, D_ATTN), BF16),
        grid=(N_HEADS_A // ATT_HG, b, s // tq),
        in_specs=[pl.BlockSpec((None, tq, gw), lambda g, bi, qi: (bi, qi, NORM_QA // gw + g)),
                  pl.BlockSpec((None, s, gw), lambda g, bi, qi: (bi, 0, NORM_KA // gw + g)),
                  pl.BlockSpec((None, s, gw), lambda g, bi, qi: (bi, 0, PLAIN_VA // gw + g)),
                  pl.BlockSpec((ATT_HG, nvar, ATT_TQ, ATT_W), lambda g, bi, qi: (g, 0, 0, 0))],
        out_specs=pl.BlockSpec((None, tq, gw), lambda g, bi, qi: (bi, qi, g)),
        compiler_params=_params(("parallel", "parallel", "arbitrary"), VMEM_LIMIT_ATTN),
        name="attn",
    )(norm3, norm3, plain3, bias)


def _mlstm_kernel(uq_ref, uk_ref, v_ref, o_ref, zif_ref, cwq_ref, cwk_ref, cbq_ref, cbk_ref, g_ref, shift_ref, out_ref,
                  c_ref, n_ref, m_ref, tq_ref, tk_ref):
    L = ML_L
    ci = pl.program_id(1)

    @pl.when(ci == 0)
    def _():
        c_ref[...] = jnp.zeros_like(c_ref)
        n_ref[...] = jnp.zeros_like(n_ref)
        m_ref[...] = jnp.zeros_like(m_ref)
        tq_ref[...] = jnp.zeros_like(tq_ref)
        tk_ref[...] = jnp.zeros_like(tk_ref)

    row8 = lax.broadcasted_iota(jnp.int32, (8, HEAD_DIM_M), 0)

    def conv_silu(u_ref, tail_ref, cw_ref, cb_ref, cols):
        u = u_ref[:, cols]
        uf = u.astype(F32)
        shifted = jnp.dot(shift_ref[...], u, preferred_element_type=F32)
        acc = cb_ref[:, cols] + uf * cw_ref[CONV_W - 1:CONV_W, cols]
        tail = tail_ref[:, cols]
        head = jnp.zeros((8, HEAD_DIM_M), F32)
        for dlt in range(1, CONV_W):
            w = cw_ref[CONV_W - 1 - dlt:CONV_W - dlt, cols]
            acc = acc + shifted[(dlt - 1) * L:dlt * L, :] * w
            head = head + jnp.where(row8 < dlt, pltpu.roll(tail, dlt, axis=0), 0.0) * w
        acc = jnp.concatenate([acc[0:8, :] + head, acc[8:, :]], axis=0)
        tail_ref[:, cols] = uf[L - 8:L, :]
        return acc * jax.nn.sigmoid(acc)

    zi = zif_ref[...]
    logf = jnp.minimum(zi, 0.0) - jnp.log1p(jnp.exp(-jnp.abs(zi)))
    row = lax.broadcasted_iota(jnp.int32, (L, L), 0)
    col = lax.broadcasted_iota(jnp.int32, (L, L), 1)
    causal = row >= col
    bcs = jnp.dot(causal.astype(F32), logf, preferred_element_type=F32,
                  precision=lax.Precision.HIGHEST)
    zi_t = zi.T
    bcs_t = bcs.T

    for hd in range(N_HEADS_M):
        cols = slice(hd * HEAD_DIM_M, (hd + 1) * HEAD_DIM_M)
        q = conv_silu(uq_ref, tq_ref, cwq_ref, cbq_ref, cols)
        k = conv_silu(uk_ref, tk_ref, cwk_ref, cbk_ref, cols) * (HEAD_DIM_M ** -0.5)
        qb = q.astype(BF16)
        kb = k.astype(BF16)
        v = v_ref[:, cols]
        i_col = zi[:, hd:hd + 1]
        b_col = bcs[:, N_HEADS_M + hd:N_HEADS_M + hd + 1]
        i_row = zi_t[hd:hd + 1, :]
        b_row = bcs_t[N_HEADS_M + hd:N_HEADS_M + hd + 1, :]
        m_prev = m_ref[hd, 0:1, 0:1]
        c_prev = c_ref[hd]
        n_prev = n_ref[hd]

        d = jnp.where(causal, b_col + (i_row - b_row), NEG)
        m_inter = b_col + m_prev
        m_t = jnp.maximum(m_inter, jnp.max(d, axis=-1, keepdims=True))
        qk = lax.dot_general(qb, kb, (((1,), (1,)), ((), ())), preferred_element_type=F32)
        sm = jnp.exp(d - m_t) * qk
        w_inter = jnp.exp(m_inter - m_t)
        num = (jnp.dot(sm.astype(BF16), v, preferred_element_type=F32)
               + w_inter * jnp.dot(qb, c_prev.astype(BF16), preferred_element_type=F32))
        den = (jnp.sum(sm, axis=-1, keepdims=True)
               + w_inter * jnp.sum(q * n_prev, axis=-1, keepdims=True))
        hh = num / jnp.maximum(jnp.abs(den), jnp.exp(-m_t))

        b_last = b_col[L - 1:L, :]
        w_end = b_last - b_col + i_col
        g_end = jnp.max(w_end, axis=0, keepdims=True)
        m_new = jnp.maximum(b_last + m_prev, g_end)
        decay = jnp.exp(b_last + m_prev - m_new)
        inj = jnp.exp(g_end - m_new)
        kw = k * jnp.exp(w_end - g_end)
        a_c = lax.dot_general(kw.astype(BF16), v, (((0,), (0,)), ((), ())), preferred_element_type=F32)
        c_ref[hd] = decay * c_prev + inj * a_c
        n_ref[hd] = decay * n_prev + inj * jnp.sum(kw, axis=0, keepdims=True)
        m_ref[hd] = jnp.broadcast_to(m_new, m_ref.shape[1:])

        hn = hh * lax.rsqrt(jnp.mean(hh * hh, axis=-1, keepdims=True) + EPS) * g_ref[:, cols]
        out_ref[:, cols] = (o_ref[:, cols].astype(F32) * hn).astype(BF16)


def _row_shift_matrices():
    t = np.arange(ML_L)
    blocks = [(t[None, :] == t[:, None] - d) for d in range(1, CONV_W)]
    return jnp.asarray(np.concatenate(blocks, axis=0), BF16)


def _mlstm(plain3, sig3, zif3, conv_w, conv_b, m_g):
    b, s, _ = plain3.shape
    L = ML_L
    dm = D_MLSTM
    return pl.pallas_call(
        _mlstm_kernel,
        out_shape=jax.ShapeDtypeStruct((b, s, dm), BF16),
        grid=(b, s // L),
        in_specs=[pl.BlockSpec((None, L, dm), lambda bi, ci: (bi, ci, PLAIN_QM // dm)),
                  pl.BlockSpec((None, L, dm), lambda bi, ci: (bi, ci, PLAIN_KM // dm)),
                  pl.BlockSpec((None, L, dm), lambda bi, ci: (bi, ci, PLAIN_VM // dm)),
                  pl.BlockSpec((None, L, dm), lambda bi, ci: (bi, ci, SIG_OM // dm)),
                  pl.BlockSpec((None, L, IF_COLS), lambda bi, ci: (bi, ci, 0)),
                  pl.BlockSpec((CONV_W, dm), lambda bi, ci: (0, 0)),
                  pl.BlockSpec((CONV_W, dm), lambda bi, ci: (0, 1)),
                  pl.BlockSpec((1, dm), lambda bi, ci: (0, 0)),
                  pl.BlockSpec((1, dm), lambda bi, ci: (0, 1)),
                  pl.BlockSpec((1, dm), lambda bi, ci: (0, 0)),
                  pl.BlockSpec(((CONV_W - 1) * L, L), lambda bi, ci: (0, 0))],
        out_specs=pl.BlockSpec((None, L, dm), lambda bi, ci: (bi, ci, 0)),
        scratch_shapes=[pltpu.VMEM((N_HEADS_M, HEAD_DIM_M, HEAD_DIM_M), F32),
                        pltpu.VMEM((N_HEADS_M, 1, HEAD_DIM_M), F32),
                        pltpu.VMEM((N_HEADS_M, 8, 128), F32),
                        pltpu.VMEM((8, dm), F32),
                        pltpu.VMEM((8, dm), F32)],
        compiler_params=_params(("parallel", "arbitrary")),
        name="mlstm",
    )(plain3, plain3, plain3, sig3, zif3, conv_w, conv_w, conv_b, conv_b, m_g, _row_shift_matrices())


def _merge_kernel(x_ref, mod_ref, a_ref, hm_ref, ga_ref, gm_ref, wa_ref, wm_ref, wo_ref, o_ref, *, row0):
    a = a_ref[...]
    hm = hm_ref[...]
    acc = None
    for j in range(D_MODEL // MERGE_TN):
        cols = slice(j * MERGE_TN, (j + 1) * MERGE_TN)
        ua = jnp.dot(a, wa_ref[:, cols], preferred_element_type=F32)
        um = jnp.dot(hm, wm_ref[:, cols], preferred_element_type=F32)
        merged = ga_ref[:, cols].astype(F32) * ua + gm_ref[:, cols].astype(F32) * um
        part = jnp.dot(merged.astype(BF16), wo_ref[cols, :], preferred_element_type=F32)
        acc = part if acc is None else acc + part
    o_ref[...] = x_ref[...] + mod_ref[row0 + 2:row0 + 3, :] * acc


def _merge(x2, mod, attn2, hm2, sig2, w_up_a, w_up_m, w_out, layer, row0, seq):
    t, d = x2.shape
    tpb = seq // MERGE_TM
    resident = pl.Buffered(1)
    return pl.pallas_call(
        functools.partial(_merge_kernel, row0=row0),
        out_shape=jax.ShapeDtypeStruct((t, d), F32),
        grid=(t // MERGE_TM,),
        in_specs=[pl.BlockSpec((MERGE_TM, d), lambda i: (i, 0)),
                  pl.BlockSpec((None, 9, d), lambda i: (i // tpb, 0, 0)),
                  pl.BlockSpec((MERGE_TM, D_ATTN), lambda i: (i, 0)),
                  pl.BlockSpec((MERGE_TM, D_MLSTM), lambda i: (i, 0)),
                  pl.BlockSpec((MERGE_TM, d), lambda i: (i, SIG_GA // d)),
                  pl.BlockSpec((MERGE_TM, d), lambda i: (i, SIG_GM // d)),
                  pl.BlockSpec((None, D_ATTN, d), lambda i: (layer, 0, 0), pipeline_mode=resident),
                  pl.BlockSpec((None, D_MLSTM, d), lambda i: (layer, 0, 0), pipeline_mode=resident),
                  pl.BlockSpec((None, d, d), lambda i: (layer, 0, 0), pipeline_mode=resident)],
        out_specs=pl.BlockSpec((MERGE_TM, d), lambda i: (i, 0)),
        compiler_params=_params(("parallel",), VMEM_LIMIT_MERGE),
        name="merge",
    )(x2, mod, attn2, hm2, sig2, sig2, w_up_a, w_up_m, w_out)


def kernel(x, c, norm_g, w_ada, b_ada, ffn1_w1, ffn1_w3, ffn1_w2, w_in, b_if, conv_w, conv_b, q_norm_g, k_norm_g,
           rel_table, m_norm_g, w_up_a, w_up_m, w_out, ffn2_w1, ffn2_w3, ffn2_w2):
    bsz, seq, d = x.shape
    depth = w_ada.shape[0]
    t = bsz * seq
    assert d == D_MODEL and seq % FFN_TM == 0 and seq % ML_L == 0 and seq % (ATT_TQ * ATT_SUB) == 0
    assert seq >= ATT_W and bsz <= 8

    c8 = jnp.zeros((8, d), F32).at[:bsz].set(c)
    mod_all = _ada(c8, w_ada, b_ada)[:, :bsz].reshape(depth, bsz, 9, d)

    fw = (_cast_pad(ffn1_w1, d, D_FF_PAD, d, CAST_B, layer=0), _cast_pad(ffn1_w3, d, D_FF_PAD, d, CAST_B, layer=0),
          _cast_pad(ffn1_w2, D_FF_PAD, d, CAST_B, d, layer=0))
    w_p, w_if = _cast_win(w_in)
    wa = _cast_pad(w_up_a, D_ATTN, d, D_ATTN, d)
    wm = _cast_pad(w_up_m, D_MLSTM, d, D_MLSTM, d)
    wo = _cast_pad(w_out, d, d, d // 2, d)

    x2 = x.reshape(t, d)
    for l in range(depth):
        mod = mod_all[l]
        x2, fw = _ffn(x2, norm_g[l, 0:1], mod, *fw, 0, seq, nxt=(ffn2_w1, ffn2_w3, ffn2_w2, l))

        bif = jnp.pad(b_if[l], (0, IF_COLS - N_IF)).reshape(1, IF_COLS)
        qk_gain = jnp.stack([q_norm_g[l] * HEAD_DIM_A ** -0.5, k_norm_g[l]]).reshape(2, 1, HEAD_DIM_A)
        sig2, norm2, plain2, zif2 = _proj(x2, norm_g[l, 1:2], mod, w_p, w_if, bif, qk_gain, l, 3, seq)
        plain3 = plain2.reshape(bsz, seq, PLAIN_COLS)
        attn = _attention(norm2.reshape(bsz, seq, NORM_COLS), plain3, _attn_bias(rel_table[l]))
        hm = _mlstm(plain3, sig2.reshape(bsz, seq, SIG_COLS), zif2.reshape(bsz, seq, IF_COLS), conv_w[l],
                    conv_b[l].reshape(1, -1), m_norm_g[l].reshape(1, -1))
        x2 = _merge(x2, mod, attn.reshape(t, D_ATTN), hm.reshape(t, D_MLSTM), sig2, wa, wm, wo, l, 3, seq)

        nxt = (ffn1_w1, ffn1_w3, ffn1_w2, l + 1) if l + 1 < depth else None
        x2, fw = _ffn(x2, norm_g[l, 2:3], mod, *fw, 6, seq, nxt=nxt)
    return x2.reshape(bsz, seq, d)
```

```python
import functools

import numpy as np
import jax
import jax.numpy as jnp
from jax import lax
from jax.experimental import pallas as pl
from jax.experimental.pallas import tpu as pltpu

F32 = jnp.float32
BF16 = jnp.bfloat16

D_MODEL = 2048
CHUNK = 64
N_HEADS_A = 8
HEAD_DIM_A = 128
D_ATTN = N_HEADS_A * HEAD_DIM_A
LEFT_CHUNKS = 8
LEFT = LEFT_CHUNKS * CHUNK
MAX_REL = 4 * CHUNK
N_HEADS_M = 4
HEAD_DIM_M = 256
D_MLSTM = N_HEADS_M * HEAD_DIM_M
CONV_W = 4
D_FF = ((8 * D_MODEL // 3 + 127) // 128) * 128
EPS = 1e-6
NEG = -1e30

SIG_GA, SIG_GM, SIG_OM = 0, D_MODEL, 2 * D_MODEL
SIG_COLS = D_MLSTM + 2 * D_MODEL
NORM_QA, NORM_KA = 0, D_ATTN
NORM_COLS = 2 * D_ATTN
PLAIN_VA, PLAIN_QM, PLAIN_KM, PLAIN_VM = 0, D_ATTN, D_ATTN + D_MLSTM, D_ATTN + 2 * D_MLSTM
PLAIN_COLS = D_ATTN + 3 * D_MLSTM
WP_SIG, WP_NORM, WP_PLAIN = 0, SIG_COLS, SIG_COLS + NORM_COLS
Z_COLS = SIG_COLS + NORM_COLS + PLAIN_COLS
IF_COLS = 128
N_IF = 2 * N_HEADS_M
WIN_OM = 3 * D_ATTN + 3 * D_MLSTM
IF0 = WIN_OM + D_MLSTM

FFN_TM, FFN_TF = 1024, 512
D_FF_PAD = ((D_FF + FFN_TF - 1) // FFN_TF) * FFN_TF
PROJ_TM = 512
PROJ_STEPS = 4
PROJ_SIG_W, PROJ_NORM_W, PROJ_PLAIN_W = SIG_COLS // PROJ_STEPS, NORM_COLS // PROJ_STEPS, PLAIN_COLS // PROJ_STEPS
ATT_TQ = 256
ATT_W = LEFT + ATT_TQ
ATT_NVAR = LEFT // ATT_TQ + 1
ATT_ROLL = 1024
ATT_HG = 4
ATT_SUB = 4
CHUNK_LOG2 = 6
assert ATT_W + ATT_TQ - 1 <= ATT_ROLL and 1 << CHUNK_LOG2 == CHUNK
ML_L = 256
MERGE_TM, MERGE_TN = 512, 512
ADA_TN = 1024
WIN_TN = 1024
WIN_J_GATE0, WIN_J_GATE1 = SIG_GA // WIN_TN, SIG_OM // WIN_TN
CAST_B = 512
MIB = 1024 * 1024
VMEM_LIMIT = 48 * MIB
VMEM_LIMIT_MERGE = 58 * MIB
VMEM_LIMIT_ATTN = 56 * MIB
VMEM_LIMIT_FFN = 60 * MIB
MOD_ROWS = 16
MOD_UNROLL = 8


def _params(sem, vmem_limit=VMEM_LIMIT):
    return pltpu.CompilerParams(dimension_semantics=sem, vmem_limit_bytes=vmem_limit)


def _modulate_into(h_ref, x_ref, g_ref, mod_ref, row0):
    x = x_ref[...]
    y = x * lax.rsqrt(jnp.mean(x * x, axis=-1, keepdims=True) + EPS)
    h = (y * g_ref[...]) * (1.0 + mod_ref[row0 + 1:row0 + 2, :]) + mod_ref[row0:row0 + 1, :]
    h_ref[...] = h.astype(BF16)


def _modulate_rows_into(h_ref, x_ref, g_ref, mod_ref, row0, p_ref):
    d = x_ref.shape[1]
    p_ref[0] = jnp.broadcast_to(g_ref[...], (MOD_ROWS, d))
    p_ref[1] = jnp.broadcast_to(1.0 + mod_ref[row0 + 1:row0 + 2, :], (MOD_ROWS, d))
    p_ref[2] = jnp.broadcast_to(mod_ref[row0:row0 + 1, :], (MOD_ROWS, d))

    def group(r, carry):
        rows = pl.ds(pl.multiple_of(r * MOD_ROWS, MOD_ROWS), MOD_ROWS)
        x = x_ref[rows, :]
        y = x * lax.rsqrt(jnp.mean(x * x, axis=-1, keepdims=True) + EPS)
        h = (y * p_ref[0]) * p_ref[1] + p_ref[2]
        h_ref[rows, :] = h.astype(BF16)
        return carry

    lax.fori_loop(0, x_ref.shape[0] // MOD_ROWS, group, 0, unroll=MOD_UNROLL)


def _ada_kernel(c_ref, w_ref, b_ref, o_ref):
    c = c_ref[...]
    ca = (c * jax.nn.sigmoid(c)).astype(BF16)
    o_ref[...] = jnp.dot(ca, w_ref[...].astype(BF16), preferred_element_type=F32) + b_ref[...]


def _ada(c8, w_ada, b_ada):
    depth, d, n = w_ada.shape
    return pl.pallas_call(
        _ada_kernel,
        out_shape=jax.ShapeDtypeStruct((depth, 8, n), F32),
        grid=(depth, n // ADA_TN),
        in_specs=[pl.BlockSpec((8, d), lambda l, j: (0, 0)),
                  pl.BlockSpec((None, d, ADA_TN), lambda l, j: (l, 0, j)),
                  pl.BlockSpec((None, 1, ADA_TN), lambda l, j: (l, 0, j))],
        out_specs=pl.BlockSpec((None, 8, ADA_TN), lambda l, j: (l, 0, j)),
        compiler_params=_params(("parallel", "parallel")),
        name="adaln",
    )(c8, w_ada, b_ada.reshape(depth, 1, n))


def _cast_kernel(w_ref, o_ref, *, rows, cols):
    br, bc = o_ref.shape
    x = w_ref[...]
    if rows % br or cols % bc:
        r = lax.broadcasted_iota(jnp.int32, (br, bc), 0) + pl.program_id(1) * br
        c = lax.broadcasted_iota(jnp.int32, (br, bc), 1) + pl.program_id(2) * bc
        x = jnp.where(jnp.logical_and(r < rows, c < cols), x, 0.0)
    o_ref[...] = x.astype(BF16)


def _cast_pad(w, rows_out, cols_out, br, bc, layer=None):
    depth, rows, cols = w.shape
    first, count = (0, depth) if layer is None else (layer, 1)
    return pl.pallas_call(
        functools.partial(_cast_kernel, rows=rows, cols=cols),
        out_shape=jax.ShapeDtypeStruct((count, rows_out, cols_out), BF16),
        grid=(count, rows_out // br, cols_out // bc),
        in_specs=[pl.BlockSpec((None, br, bc), lambda l, i, j: (first + l, i, j))],
        out_specs=pl.BlockSpec((None, br, bc), lambda l, i, j: (l, i, j)),
        compiler_params=_params(("parallel", "parallel", "parallel")),
        name="cast",
    )(w)


def _cast_win_kernel(a_ref, b_ref, wp_ref, wif_ref):
    j = pl.program_id(1)
    gates = jnp.logical_and(j >= WIN_J_GATE0, j < WIN_J_GATE1)

    @pl.when(jnp.logical_not(gates))
    def _():
        wp_ref[...] = a_ref[...].T.astype(BF16)

    @pl.when(gates)
    def _():
        wp_ref[...] = jnp.concatenate([a_ref[N_IF:, :], b_ref[...]], axis=0).T.astype(BF16)

    @pl.when(j == WIN_J_GATE0)
    def _():
        row = lax.broadcasted_iota(jnp.int32, (IF_COLS, a_ref.shape[1]), 0)
        wif_ref[...] = jnp.where(row < N_IF, a_ref[0:IF_COLS, :], 0.0).T.astype(BF16)


def _win_block(j):
    j_om = WIN_OM // WIN_TN
    return jnp.where(j < WIN_J_GATE1, j + j_om + 1, jnp.where(j == WIN_J_GATE1, j_om, j - (WIN_J_GATE1 + 1)))


def _cast_win(w_in):
    depth, d, _ = w_in.shape
    w_t = jnp.swapaxes(w_in, 1, 2)
    per = WIN_TN // N_IF

    def next_block(l, j):
        jg = jnp.clip(j, WIN_J_GATE0, WIN_J_GATE1 - 1)
        return (l, (_win_block(jg) + 1) * per, 0)

    return pl.pallas_call(
        _cast_win_kernel,
        out_shape=(jax.ShapeDtypeStruct((depth, d, Z_COLS), BF16), jax.ShapeDtypeStruct((depth, d, IF_COLS), BF16)),
        grid=(depth, Z_COLS // WIN_TN),
        in_specs=[pl.BlockSpec((None, WIN_TN, d), lambda l, j: (l, _win_block(j), 0)),
                  pl.BlockSpec((None, N_IF, d), next_block)],
        out_specs=(pl.BlockSpec((None, d, WIN_TN), lambda l, j: (l, 0, j)),
                   pl.BlockSpec((None, d, IF_COLS), lambda l, j: (l, 0, 0))),
        compiler_params=_params(("parallel", "arbitrary")),
        name="cast_win",
    )(w_t, w_t)


def _ffn_kernel(x_ref, g_ref, mod_ref, w1_ref, w3_ref, w2_ref, *rest, row0, d_ff_next):
    if d_ff_next is None:
        o_ref, h_ref, p_ref = rest
    else:
        n1_ref, n3_ref, n2_ref, o_ref, c1_ref, c3_ref, c2_ref, h_ref, p_ref = rest
    j = pl.program_id(1)

    @pl.when(j == 0)
    def _():
        _modulate_rows_into(h_ref, x_ref, g_ref, mod_ref, row0, p_ref)

    h = h_ref[...]
    a = jnp.dot(h, w1_ref[...], preferred_element_type=F32)
    b = jnp.dot(h, w3_ref[...], preferred_element_type=F32)
    if d_ff_next is not None:
        tf = c1_ref.shape[1]
        hid = j * tf + lax.broadcasted_iota(jnp.int32, (1, tf), 1)
        c1_ref[...] = jnp.where(hid < d_ff_next, n1_ref[...], 0.0).astype(BF16)
        c3_ref[...] = jnp.where(hid < d_ff_next, n3_ref[...], 0.0).astype(BF16)
        hid = j * tf + lax.broadcasted_iota(jnp.int32, (tf, 1), 0)
        c2_ref[...] = jnp.where(hid < d_ff_next, n2_ref[...], 0.0).astype(BF16)
    u = (a * jax.nn.sigmoid(a)) * b
    prev = jnp.where(j == 0, 0.0, o_ref[...])
    o_ref[...] = prev + jnp.dot(u.astype(BF16), w2_ref[...], preferred_element_type=F32)

    @pl.when(j == pl.num_programs(1) - 1)
    def _():
        o_ref[...] = x_ref[...] + (0.5 * mod_ref[row0 + 2:row0 + 3, :]) * o_ref[...]


def _ffn(x2, g, mod, w1, w3, w2, row0, seq, nxt=None):
    t, d = x2.shape
    tpb = seq // FFN_TM
    ni = t // FFN_TM
    nf = w1.shape[2] // FFN_TF
    in_specs = [pl.BlockSpec((FFN_TM, d), lambda i, j: (i, 0)),
                pl.BlockSpec((1, d), lambda i, j: (0, 0)),
                pl.BlockSpec((None, 9, d), lambda i, j: (i // tpb, 0, 0)),
                pl.BlockSpec((None, d, FFN_TF), lambda i, j: (0, 0, j)),
                pl.BlockSpec((None, d, FFN_TF), lambda i, j: (0, 0, j)),
                pl.BlockSpec((None, FFN_TF, d), lambda i, j: (0, j, 0))]
    out_shape = [jax.ShapeDtypeStruct((t, d), F32)]
    out_specs = [pl.BlockSpec((FFN_TM, d), lambda i, j: (i, 0))]
    args = [x2, g, mod, w1, w3, w2]
    d_ff_next = None
    if nxt is not None:
        n1, n3, n2, lyr = nxt
        d_ff_next = n1.shape[2]
        assert d % ni == 0 and pl.cdiv(d_ff_next, FFN_TF) == nf
        cr = d // ni
        in_specs += [pl.BlockSpec((None, cr, FFN_TF), lambda i, j: (lyr, i, j)),
                     pl.BlockSpec((None, cr, FFN_TF), lambda i, j: (lyr, i, j)),
                     pl.BlockSpec((None, FFN_TF, cr), lambda i, j: (lyr, j, i))]
        out_shape += [jax.ShapeDtypeStruct((1, d, nf * FFN_TF), BF16), jax.ShapeDtypeStruct((1, d, nf * FFN_TF), BF16),
                      jax.ShapeDtypeStruct((1, nf * FFN_TF, d), BF16)]
        out_specs += [pl.BlockSpec((None, cr, FFN_TF), lambda i, j: (0, i, j)),
                      pl.BlockSpec((None, cr, FFN_TF), lambda i, j: (0, i, j)),
                      pl.BlockSpec((None, FFN_TF, cr), lambda i, j: (0, j, i))]
        args += [n1, n3, n2]
    res = pl.pallas_call(
        functools.partial(_ffn_kernel, row0=row0, d_ff_next=d_ff_next),
        out_shape=tuple(out_shape),
        grid=(ni, nf),
        in_specs=in_specs,
        out_specs=tuple(out_specs),
        scratch_shapes=[pltpu.VMEM((FFN_TM, d), BF16), pltpu.VMEM((3, MOD_ROWS, d), F32)],
        compiler_params=_params(("parallel", "arbitrary"), VMEM_LIMIT_FFN),
        name="ffn",
    )(*args)
    return res[0], tuple(res[1:])


def _proj_kernel(x_ref, g_ref, mod_ref, ws_ref, wn_ref, wp_ref, wif_ref, bif_ref, gain_ref, ua_ref, um_ref, uo_ref,
                 sig_ref, norm_ref, plain_ref, zif_ref, ca_ref, cm_ref, co_ref, h_ref, *, row0):
    ca_ref[...] = ua_ref[...].astype(BF16)
    cm_ref[...] = um_ref[...].astype(BF16)
    co_ref[...] = uo_ref[...].astype(BF16)

    @pl.when(pl.program_id(1) == 0)
    def _():
        _modulate_into(h_ref, x_ref, g_ref, mod_ref, row0)
        zif_ref[...] = jnp.dot(h_ref[...], wif_ref[...], preferred_element_type=F32) + bif_ref[...]

    h = h_ref[...]
    sig_ref[...] = jax.nn.sigmoid(jnp.dot(h, ws_ref[...], preferred_element_type=F32)).astype(BF16)
    qk = jnp.dot(h, wn_ref[...], preferred_element_type=F32)
    for hd in range(PROJ_NORM_W // HEAD_DIM_A):
        cols = slice(hd * HEAD_DIM_A, (hd + 1) * HEAD_DIM_A)
        a = qk[:, cols]
        y = a * lax.rsqrt(jnp.mean(a * a, axis=-1, keepdims=True) + EPS)
        norm_ref[:, cols] = (y * gain_ref[...]).astype(BF16)
    plain_ref[...] = jnp.dot(h, wp_ref[...], preferred_element_type=F32).astype(BF16)


def _proj(x2, g, mod, w_p, w_if, b_if, qk_gain, w_up_a, w_up_m, w_out, layer, row0, seq):
    t, d = x2.shape
    tpb = seq // PROJ_TM
    ni = t // PROJ_TM
    sw, nw, pw = PROJ_SIG_W, PROJ_NORM_W, PROJ_PLAIN_W
    q_steps = D_ATTN // nw
    ra, ro, cw = D_ATTN // ni, d // ni, d // PROJ_STEPS
    assert D_ATTN == D_MLSTM and D_ATTN % ni == 0 and ra % 16 == 0

    def wspec(rows, lyr):
        return pl.BlockSpec((None, rows, cw), lambda i, j: (lyr, i, j))

    return pl.pallas_call(
        functools.partial(_proj_kernel, row0=row0),
        out_shape=(jax.ShapeDtypeStruct((t, SIG_COLS), BF16), jax.ShapeDtypeStruct((t, NORM_COLS), BF16),
                   jax.ShapeDtypeStruct((t, PLAIN_COLS), BF16), jax.ShapeDtypeStruct((t, IF_COLS), F32),
                   jax.ShapeDtypeStruct((1, D_ATTN, d), BF16), jax.ShapeDtypeStruct((1, D_MLSTM, d), BF16),
                   jax.ShapeDtypeStruct((1, d, d), BF16)),
        grid=(ni, PROJ_STEPS),
        in_specs=[pl.BlockSpec((PROJ_TM, d), lambda i, j: (i, 0)),
                  pl.BlockSpec((1, d), lambda i, j: (0, 0)),
                  pl.BlockSpec((None, 9, d), lambda i, j: (i // tpb, 0, 0)),
                  pl.BlockSpec((None, d, sw), lambda i, j: (layer, 0, WP_SIG // sw + j)),
                  pl.BlockSpec((None, d, nw), lambda i, j: (layer, 0, WP_NORM // nw + j)),
                  pl.BlockSpec((None, d, pw), lambda i, j: (layer, 0, WP_PLAIN // pw + j)),
                  pl.BlockSpec((None, d, IF_COLS), lambda i, j: (layer, 0, 0)),
                  pl.BlockSpec((1, IF_COLS), lambda i, j: (0, 0)),
                  pl.BlockSpec((None, 1, HEAD_DIM_A), lambda i, j: (j // q_steps, 0, 0)),
                  wspec(ra, layer), wspec(ra, layer), wspec(ro, layer)],
        out_specs=(pl.BlockSpec((PROJ_TM, sw), lambda i, j: (i, j)),
                   pl.BlockSpec((PROJ_TM, nw), lambda i, j: (i, j)),
                   pl.BlockSpec((PROJ_TM, pw), lambda i, j: (i, j)),
                   pl.BlockSpec((PROJ_TM, IF_COLS), lambda i, j: (i, 0)),
                   wspec(ra, 0), wspec(ra, 0), wspec(ro, 0)),
        scratch_shapes=[pltpu.VMEM((PROJ_TM, d), BF16)],
        compiler_params=_params(("parallel", "arbitrary")),
        name="proj",
    )(x2, g, mod, w_p, w_p, w_p, w_if, b_if, qk_gain, w_up_a, w_up_m, w_out)


def _bias_kernel(g_ref, o_ref):
    r = lax.broadcasted_iota(jnp.int32, (ATT_TQ, ATT_W), 0)
    c = lax.broadcasted_iota(jnp.int32, (ATT_TQ, ATT_W), 1)
    for v in range(ATT_NVAR):
        x = jnp.broadcast_to(g_ref[v:v + 1, :], (ATT_TQ, ATT_ROLL))
        y = pltpu.roll(x, 0, 1, stride=1, stride_axis=0)[:, :ATT_W]
        dq = jnp.right_shift(r + v * ATT_TQ, CHUNK_LOG2) - jnp.right_shift(c, CHUNK_LOG2)
        valid = jnp.logical_and(dq >= 0, dq <= LEFT_CHUNKS)
        o_ref[v] = jnp.where(valid, y, NEG)


def _attn_bias(table):
    h = table.shape[0]
    p = np.arange(ATT_ROLL)
    u = np.where(p < ATT_W, p, p - ATT_ROLL)
    idx = np.stack([np.clip(v * ATT_TQ - u, -(CHUNK - 1), MAX_REL) + (CHUNK - 1) for v in range(ATT_NVAR)])
    g = jnp.take(table, jnp.asarray(idx.reshape(-1), jnp.int32), axis=1).reshape(h, ATT_NVAR, ATT_ROLL)
    return pl.pallas_call(
        _bias_kernel,
        out_shape=jax.ShapeDtypeStruct((h, ATT_NVAR, ATT_TQ, ATT_W), F32),
        grid=(h,),
        in_specs=[pl.BlockSpec((None, ATT_NVAR, ATT_ROLL), lambda i: (i, 0, 0))],
        out_specs=pl.BlockSpec((None, ATT_NVAR, ATT_TQ, ATT_W), lambda i: (i, 0, 0, 0)),
        compiler_params=_params(("parallel",)),
        name="attn_bias",
    )(g)


def _attn_kernel(q_ref, k_ref, v_ref, bias_ref, o_ref):
    units = [(sub, hd) for sub in range(ATT_SUB) for hd in range(ATT_HG)]

    def window(sub):
        qi = pl.program_id(2) * ATT_SUB + sub
        var = jnp.minimum(qi, ATT_NVAR - 1)
        return var, pl.multiple_of((qi - var) * ATT_TQ, ATT_TQ)

    def scores(sub, hd):
        var, ks = window(sub)
        cols = slice(hd * HEAD_DIM_A, (hd + 1) * HEAD_DIM_A)
        q = q_ref[sub * ATT_TQ:(sub + 1) * ATT_TQ, cols]
        k = k_ref[pl.ds(ks, ATT_W), cols]
        s = lax.dot_general(q, k, (((1,), (1,)), ((), ())), preferred_element_type=F32)
        return s + bias_ref[hd, var]

    s_next = scores(*units[0])
    for n, (sub, hd) in enumerate(units):
        cols = slice(hd * HEAD_DIM_A, (hd + 1) * HEAD_DIM_A)
        s = s_next
        if n + 1 < len(units):
            s_next = scores(*units[n + 1])
        p = jnp.exp(s - jnp.max(s, axis=-1, keepdims=True))
        l = jnp.sum(p, axis=-1, keepdims=True)
        _, ks = window(sub)
        v = v_ref[pl.ds(ks, ATT_W), cols]
        o = jnp.dot(p.astype(BF16), v, preferred_element_type=F32)
        o_ref[sub * ATT_TQ:(sub + 1) * ATT_TQ, cols] = (o / l).astype(BF16)


def _attention(norm3, plain3, bias):
    b, s, _ = norm3.shape
    gw = ATT_HG * HEAD_DIM_A
    tq = ATT_TQ * ATT_SUB
    nvar = bias.shape[1]
    return pl.pallas_call(
        _attn_kernel,
        out_shape=jax.ShapeDtypeStruct((b, s, D_ATTN), BF16),
        grid=(N_HEADS_A // ATT_HG, b, s // tq),
        in_specs=[pl.BlockSpec((None, tq, gw), lambda g, bi, qi: (bi, qi, NORM_QA // gw + g)),
                  pl.BlockSpec((None, s, gw), lambda g, bi, qi: (bi, 0, NORM_KA // gw + g)),
                  pl.BlockSpec((None, s, gw), lambda g, bi, qi: (bi, 0, PLAIN_VA // gw + g)),
                  pl.BlockSpec((ATT_HG, nvar, ATT_TQ, ATT_W), lambda g, bi, qi: (g, 0, 0, 0))],
        out_specs=pl.BlockSpec((None, tq, gw), lambda g, bi, qi: (bi, qi, g)),
        compiler_params=_params(("parallel", "parallel", "arbitrary"), VMEM_LIMIT_ATTN),
        name="attn",
    )(norm3, norm3, plain3, bias)


def _mlstm_kernel(uq_ref, uk_ref, v_ref, o_ref, zif_ref, cwq_ref, cwk_ref, cbq_ref, cbk_ref, g_ref, shift_ref, out_ref,
                  c_ref, n_ref, m_ref, tq_ref, tk_ref):
    L = ML_L
    ci = pl.program_id(1)

    @pl.when(ci == 0)
    def _():
        c_ref[...] = jnp.zeros_like(c_ref)
        n_ref[...] = jnp.zeros_like(n_ref)
        m_ref[...] = jnp.zeros_like(m_ref)
        tq_ref[...] = jnp.zeros_like(tq_ref)
        tk_ref[...] = jnp.zeros_like(tk_ref)

    row8 = lax.broadcasted_iota(jnp.int32, (8, HEAD_DIM_M), 0)

    def conv_silu(u_ref, tail_ref, cw_ref, cb_ref, cols):
        u = u_ref[:, cols]
        uf = u.astype(F32)
        shifted = jnp.dot(shift_ref[...], u, preferred_element_type=F32)
        acc = cb_ref[:, cols] + uf * cw_ref[CONV_W - 1:CONV_W, cols]
        tail = tail_ref[:, cols]
        head = jnp.zeros((8, HEAD_DIM_M), F32)
        for dlt in range(1, CONV_W):
            w = cw_ref[CONV_W - 1 - dlt:CONV_W - dlt, cols]
            acc = acc + shifted[(dlt - 1) * L:dlt * L, :] * w
            head = head + jnp.where(row8 < dlt, pltpu.roll(tail, dlt, axis=0), 0.0) * w
        acc = jnp.concatenate([acc[0:8, :] + head, acc[8:, :]], axis=0)
        tail_ref[:, cols] = uf[L - 8:L, :]
        return acc * jax.nn.sigmoid(acc)

    zi = zif_ref[...]
    logf = jnp.minimum(zi, 0.0) - jnp.log1p(jnp.exp(-jnp.abs(zi)))
    row = lax.broadcasted_iota(jnp.int32, (L, L), 0)
    col = lax.broadcasted_iota(jnp.int32, (L, L), 1)
    causal = row >= col
    bcs = jnp.dot(causal.astype(F32), logf, preferred_element_type=F32,
                  precision=lax.Precision.HIGHEST)
    zi_t = zi.T
    bcs_t = bcs.T

    for hd in range(N_HEADS_M):
        cols = slice(hd * HEAD_DIM_M, (hd + 1) * HEAD_DIM_M)
        q = conv_silu(uq_ref, tq_ref, cwq_ref, cbq_ref, cols)
        k = conv_silu(uk_ref, tk_ref, cwk_ref, cbk_ref, cols) * (HEAD_DIM_M ** -0.5)
        qb = q.astype(BF16)
        kb = k.astype(BF16)
        v = v_ref[:, cols]
        i_col = zi[:, hd:hd + 1]
        b_col = bcs[:, N_HEADS_M + hd:N_HEADS_M + hd + 1]
        i_row = zi_t[hd:hd + 1, :]
        b_row = bcs_t[N_HEADS_M + hd:N_HEADS_M + hd + 1, :]
        m_prev = m_ref[hd, 0:1, 0:1]
        c_prev = c_ref[hd]
        n_prev = n_ref[hd]

        d = jnp.where(causal, b_col + (i_row - b_row), NEG)
        m_inter = b_col + m_prev
        m_t = jnp.maximum(m_inter, jnp.max(d, axis=-1, keepdims=True))
        qk = lax.dot_general(qb, kb, (((1,), (1,)), ((), ())), preferred_element_type=F32)
        sm = jnp.exp(d - m_t) * qk
        w_inter = jnp.exp(m_inter - m_t)
        num = (jnp.dot(sm.astype(BF16), v, preferred_element_type=F32)
               + w_inter * jnp.dot(qb, c_prev.astype(BF16), preferred_element_type=F32))
        den = (jnp.sum(sm, axis=-1, keepdims=True)
               + w_inter * jnp.sum(q * n_prev, axis=-1, keepdims=True))
        hh = num / jnp.maximum(jnp.abs(den), jnp.exp(-m_t))

        b_last = b_col[L - 1:L, :]
        w_end = b_last - b_col + i_col
        g_end = jnp.max(w_end, axis=0, keepdims=True)
        m_new = jnp.maximum(b_last + m_prev, g_end)
        decay = jnp.exp(b_last + m_prev - m_new)
        inj = jnp.exp(g_end - m_new)
        kw = k * jnp.exp(w_end - g_end)
        a_c = lax.dot_general(kw.astype(BF16), v, (((0,), (0,)), ((), ())), preferred_element_type=F32)
        c_ref[hd] = decay * c_prev + inj * a_c
        n_ref[hd] = decay * n_prev + inj * jnp.sum(kw, axis=0, keepdims=True)
        m_ref[hd] = jnp.broadcast_to(m_new, m_ref.shape[1:])

        hn = hh * lax.rsqrt(jnp.mean(hh * hh, axis=-1, keepdims=True) + EPS) * g_ref[:, cols]
        out_ref[:, cols] = (o_ref[:, cols].astype(F32) * hn).astype(BF16)


def _row_shift_matrices():
    t = np.arange(ML_L)
    blocks = [(t[None, :] == t[:, None] - d) for d in range(1, CONV_W)]
    return jnp.asarray(np.concatenate(blocks, axis=0), BF16)


def _mlstm(plain3, sig3, zif3, conv_w, conv_b, m_g):
    b, s, _ = plain3.shape
    L = ML_L
    dm = D_MLSTM
    return pl.pallas_call(
        _mlstm_kernel,
        out_shape=jax.ShapeDtypeStruct((b, s, dm), BF16),
        grid=(b, s // L),
        in_specs=[pl.BlockSpec((None, L, dm), lambda bi, ci: (bi, ci, PLAIN_QM // dm)),
                  pl.BlockSpec((None, L, dm), lambda bi, ci: (bi, ci, PLAIN_KM // dm)),
                  pl.BlockSpec((None, L, dm), lambda bi, ci: (bi, ci, PLAIN_VM // dm)),
                  pl.BlockSpec((None, L, dm), lambda bi, ci: (bi, ci, SIG_OM // dm)),
                  pl.BlockSpec((None, L, IF_COLS), lambda bi, ci: (bi, ci, 0)),
                  pl.BlockSpec((CONV_W, dm), lambda bi, ci: (0, 0)),
                  pl.BlockSpec((CONV_W, dm), lambda bi, ci: (0, 1)),
                  pl.BlockSpec((1, dm), lambda bi, ci: (0, 0)),
                  pl.BlockSpec((1, dm), lambda bi, ci: (0, 1)),
                  pl.BlockSpec((1, dm), lambda bi, ci: (0, 0)),
                  pl.BlockSpec(((CONV_W - 1) * L, L), lambda bi, ci: (0, 0))],
        out_specs=pl.BlockSpec((None, L, dm), lambda bi, ci: (bi, ci, 0)),
        scratch_shapes=[pltpu.VMEM((N_HEADS_M, HEAD_DIM_M, HEAD_DIM_M), F32),
                        pltpu.VMEM((N_HEADS_M, 1, HEAD_DIM_M), F32),
                        pltpu.VMEM((N_HEADS_M, 8, 128), F32),
                        pltpu.VMEM((8, dm), F32),
                        pltpu.VMEM((8, dm), F32)],
        compiler_params=_params(("parallel", "arbitrary")),
        name="mlstm",
    )(plain3, plain3, plain3, sig3, zif3, conv_w, conv_w, conv_b, conv_b, m_g, _row_shift_matrices())


def _merge_kernel(x_ref, mod_ref, a_ref, hm_ref, ga_ref, gm_ref, wa_ref, wm_ref, wo_ref, o_ref, *, row0):
    a = a_ref[...]
    hm = hm_ref[...]
    acc = None
    for j in range(D_MODEL // MERGE_TN):
        cols = slice(j * MERGE_TN, (j + 1) * MERGE_TN)
        ua = jnp.dot(a, wa_ref[:, cols], preferred_element_type=F32)
        um = jnp.dot(hm, wm_ref[:, cols], preferred_element_type=F32)
        merged = ga_ref[:, cols].astype(F32) * ua + gm_ref[:, cols].astype(F32) * um
        part = jnp.dot(merged.astype(BF16), wo_ref[cols, :], preferred_element_type=F32)
        acc = part if acc is None else acc + part
    o_ref[...] = x_ref[...] + mod_ref[row0 + 2:row0 + 3, :] * acc


def _merge(x2, mod, attn2, hm2, sig2, w_up_a, w_up_m, w_out, row0, seq):
    t, d = x2.shape
    tpb = seq // MERGE_TM
    resident = pl.Buffered(1)
    return pl.pallas_call(
        functools.partial(_merge_kernel, row0=row0),
        out_shape=jax.ShapeDtypeStruct((t, d), F32),
        grid=(t // MERGE_TM,),
        in_specs=[pl.BlockSpec((MERGE_TM, d), lambda i: (i, 0)),
                  pl.BlockSpec((None, 9, d), lambda i: (i // tpb, 0, 0)),
                  pl.BlockSpec((MERGE_TM, D_ATTN), lambda i: (i, 0)),
                  pl.BlockSpec((MERGE_TM, D_MLSTM), lambda i: (i, 0)),
                  pl.BlockSpec((MERGE_TM, d), lambda i: (i, SIG_GA // d)),
                  pl.BlockSpec((MERGE_TM, d), lambda i: (i, SIG_GM // d)),
                  pl.BlockSpec((None, D_ATTN, d), lambda i: (0, 0, 0), pipeline_mode=resident),
                  pl.BlockSpec((None, D_MLSTM, d), lambda i: (0, 0, 0), pipeline_mode=resident),
                  pl.BlockSpec((None, d, d), lambda i: (0, 0, 0), pipeline_mode=resident)],
        out_specs=pl.BlockSpec((MERGE_TM, d), lambda i: (i, 0)),
        compiler_params=_params(("parallel",), VMEM_LIMIT_MERGE),
        name="merge",
    )(x2, mod, attn2, hm2, sig2, sig2, w_up_a, w_up_m, w_out)


def kernel(x, c, norm_g, w_ada, b_ada, ffn1_w1, ffn1_w3, ffn1_w2, w_in, b_if, conv_w, conv_b, q_norm_g, k_norm_g,
           rel_table, m_norm_g, w_up_a, w_up_m, w_out, ffn2_w1, ffn2_w3, ffn2_w2):
    bsz, seq, d = x.shape
    depth = w_ada.shape[0]
    t = bsz * seq
    assert d == D_MODEL and seq % FFN_TM == 0 and seq % ML_L == 0 and seq % (ATT_TQ * ATT_SUB) == 0
    assert seq >= ATT_W and bsz <= 8

    c8 = jnp.zeros((8, d), F32).at[:bsz].set(c)
    mod_all = _ada(c8, w_ada, b_ada)[:, :bsz].reshape(depth, bsz, 9, d)

    fw = (_cast_pad(ffn1_w1, d, D_FF_PAD, d, CAST_B, layer=0), _cast_pad(ffn1_w3, d, D_FF_PAD, d, CAST_B, layer=0),
          _cast_pad(ffn1_w2, D_FF_PAD, d, CAST_B, d, layer=0))
    w_p, w_if = _cast_win(w_in)

    x2 = x.reshape(t, d)
    for l in range(depth):
        mod = mod_all[l]
        x2, fw = _ffn(x2, norm_g[l, 0:1], mod, *fw, 0, seq, nxt=(ffn2_w1, ffn2_w3, ffn2_w2, l))

        bif = jnp.pad(b_if[l], (0, IF_COLS - N_IF)).reshape(1, IF_COLS)
        qk_gain = jnp.stack([q_norm_g[l] * HEAD_DIM_A ** -0.5, k_norm_g[l]]).reshape(2, 1, HEAD_DIM_A)
        sig2, norm2, plain2, zif2, wa, wm, wo = _proj(x2, norm_g[l, 1:2], mod, w_p, w_if, bif, qk_gain,
                                                      w_up_a, w_up_m, w_out, l, 3, seq)
        plain3 = plain2.reshape(bsz, seq, PLAIN_COLS)
        attn = _attention(norm2.reshape(bsz, seq, NORM_COLS), plain3, _attn_bias(rel_table[l]))
        hm = _mlstm(plain3, sig2.reshape(bsz, seq, SIG_COLS), zif2.reshape(bsz, seq, IF_COLS), conv_w[l],
                    conv_b[l].reshape(1, -1), m_norm_g[l].reshape(1, -1))
        x2 = _merge(x2, mod, attn.reshape(t, D_ATTN), hm.reshape(t, D_MLSTM), sig2, wa, wm, wo, 3, seq)

        nxt = (ffn1_w1, ffn1_w3, ffn1_w2, l + 1) if l + 1 < depth else None
        x2, fw = _ffn(x2, norm_g[l, 2:3], mod, *fw, 6, seq, nxt=nxt)
    return x2.reshape(bsz, seq, d)
```
